```python
import math
import jax, jax.numpy as jnp
from jax import lax
import numpy as np

D_MODEL = 1024
BATCH = 8
SEQ = 2048
DEPTH = 4

N_MIXERS = 2
N_HEADS = 16
HEAD_DIM = D_MODEL // N_HEADS
ROPE_THETA = 10000.0
MOBA_BLOCK = 256
MOBA_TOPK = 3
Q_CHUNK = 16
CONV_WIDTH = 31
FFN_CONV_WIDTH = 3
D_FF = 2816
NORM_EPS = 1e-6
LN_EPS = 1e-5
N_ATTN_LAYERS = (DEPTH + 1) // 2
N_CONV_LAYERS = DEPTH // 2

kernel_name = "moba_conformer_convffn_hybrid"


def rmsnorm(x, g):
    xf = x.astype(jnp.float32)
    y = xf * lax.rsqrt(jnp.mean(xf * xf, axis=-1, keepdims=True) + NORM_EPS)
    return (y * g.astype(jnp.float32)).astype(x.dtype)


def layernorm(x, g, b):
    xf = x.astype(jnp.float32)
    mu = jnp.mean(xf, axis=-1, keepdims=True)
    var = jnp.mean(jnp.square(xf - mu), axis=-1, keepdims=True)
    y = (xf - mu) * lax.rsqrt(var + LN_EPS)
    return (y * g.astype(jnp.float32) + b.astype(jnp.float32)).astype(x.dtype)


def causal_dwconv(x, w, b):
    width, ch = w.shape
    y = lax.conv_general_dilated(x, w[:, None, :], window_strides=(1,), padding=[(width - 1, 0)],
                                 dimension_numbers=("NWC", "WIO", "NWC"), feature_group_count=ch)
    return y + b


def rotary(x, cos, sin):
    xf = x.astype(jnp.float32)
    x1, x2 = jnp.split(xf, 2, axis=-1)
    return jnp.concatenate([x1 * cos - x2 * sin, x2 * cos + x1 * sin], axis=-1).astype(x.dtype)


def moba_attention(h, w_qkv, w_o):
    B, S, _ = h.shape
    qkv = (h @ w_qkv).reshape(B, S, 3, N_HEADS, HEAD_DIM)
    q = jnp.transpose(qkv[:, :, 0], (0, 2, 1, 3))
    k = jnp.transpose(qkv[:, :, 1], (0, 2, 1, 3))
    v = jnp.transpose(qkv[:, :, 2], (0, 2, 1, 3))
    pos = jnp.arange(S, dtype=jnp.float32)
    inv_freq = 1.0 / (ROPE_THETA ** (jnp.arange(0, HEAD_DIM, 2, dtype=jnp.float32) / HEAD_DIM))
    ang = pos[:, None] * inv_freq[None, :]
    cos, sin = jnp.cos(ang), jnp.sin(ang)
    q = rotary(q, cos, sin) * (HEAD_DIM ** -0.5)
    k = rotary(k, cos, sin)

    nb = -(-S // MOBA_BLOCK)
    sp = nb * MOBA_BLOCK
    padw = ((0, 0), (0, 0), (0, sp - S), (0, 0))
    q, k, v = jnp.pad(q, padw), jnp.pad(k, padw), jnp.pad(v, padw)
    kb = k.reshape(B, N_HEADS, nb, MOBA_BLOCK, HEAD_DIM)
    vb = v.reshape(B, N_HEADS, nb, MOBA_BLOCK, HEAD_DIM)
    kmean = jnp.mean(kb.astype(jnp.float32), axis=3)
    n_sel = min(MOBA_TOPK, nb)
    n_chunks = sp // Q_CHUNK
    qc = jnp.moveaxis(q.reshape(B, N_HEADS, n_chunks, Q_CHUNK, HEAD_DIM), 2, 0)

    b_idx = jnp.arange(B)[:, None, None, None]
    h_idx = jnp.arange(N_HEADS)[None, :, None, None]
    blk_ids = jnp.arange(nb)
    slot_ids = jnp.arange(n_sel)
    q_off = jnp.arange(Q_CHUNK)
    k_off = jnp.arange(MOBA_BLOCK)

    def attend_chunk(args):
        q_c, c = args
        start = c * Q_CHUNK
        blk = start // MOBA_BLOCK
        gate = jnp.einsum("bhqd,bhnd->bhqn", q_c.astype(jnp.float32), kmean)
        gate = jnp.where(blk_ids < blk, gate, -jnp.inf)
        _, sel = lax.top_k(gate, n_sel)
        k_sel = kb[b_idx, h_idx, sel]
        v_sel = vb[b_idx, h_idx, sel]
        s_sel = jnp.einsum("bhqd,bhqnkd->bhqnk", q_c, k_sel).astype(jnp.float32)
        s_sel = jnp.where((slot_ids < blk)[:, None], s_sel, -jnp.inf)
        k_own = lax.dynamic_index_in_dim(kb, blk, axis=2, keepdims=False)
        v_own = lax.dynamic_index_in_dim(vb, blk, axis=2, keepdims=False)
        s_own = jnp.einsum("bhqd,bhkd->bhqk", q_c, k_own).astype(jnp.float32)
        causal = (blk * MOBA_BLOCK + k_off)[None, :] <= (start + q_off)[:, None]
        s_own = jnp.where(causal, s_own, -jnp.inf)
        scores = jnp.concatenate([s_sel.reshape(B, N_HEADS, Q_CHUNK, n_sel * MOBA_BLOCK), s_own], axis=-1)
        p = jax.nn.softmax(scores, axis=-1).astype(v.dtype)
        p_sel = p[..., : n_sel * MOBA_BLOCK].reshape(B, N_HEADS, Q_CHUNK, n_sel, MOBA_BLOCK)
        p_own = p[..., n_sel * MOBA_BLOCK:]
        return (jnp.einsum("bhqnk,bhqnkd->bhqd", p_sel, v_sel)
                + jnp.einsum("bhqk,bhkd->bhqd", p_own, v_own))

    out = lax.map(attend_chunk, (qc, jnp.arange(n_chunks)))
    out = jnp.moveaxis(out, 0, 2).reshape(B, N_HEADS, sp, HEAD_DIM)[:, :, :S]
    out = jnp.transpose(out, (0, 2, 1, 3)).reshape(B, S, N_HEADS * HEAD_DIM)
    return out @ w_o


def conformer_conv(h, w_pw1, b_pw1, w_dw, b_dw, ln_g, ln_b, w_pw2, b_pw2):
    u = h @ w_pw1 + b_pw1
    a, g = jnp.split(u, 2, axis=-1)
    u = a * jax.nn.sigmoid(g)
    u = causal_dwconv(u, w_dw, b_dw)
    u = layernorm(u, ln_g, ln_b)
    u = jax.nn.silu(u)
    return u @ w_pw2 + b_pw2


def conv_ffn(h, w_up, w_dw, b_dw, w_down):
    u = causal_dwconv(h @ w_up, w_dw, b_dw)
    gate, val = jnp.split(u, 2, axis=-1)
    return (jax.nn.silu(gate) * val) @ w_down


def setup_inputs(seed: int = 0) -> dict:
    key = jax.random.key(seed)
    ks = jax.random.split(key, 20)
    D, F = D_MODEL, D_FF
    na, nc = N_ATTN_LAYERS, N_CONV_LAYERS
    nrm = lambda k, shape, fan_in: jax.random.normal(k, shape, jnp.float32) * (fan_in ** -0.5)
    small = lambda k, shape, s: jax.random.normal(k, shape, jnp.float32) * s
    return {
        "x": jax.random.normal(ks[0], (BATCH, SEQ, D), jnp.float32),
        "norm_mix_g": 1.0 + small(ks[1], (DEPTH, D), 0.1),
        "norm_ffn_g": 1.0 + small(ks[2], (DEPTH, D), 0.1),
        "final_norm_g": 1.0 + small(ks[3], (D,), 0.1),
        "attn_w_qkv": nrm(ks[4], (na, D, 3 * D), D),
        "attn_w_o": nrm(ks[5], (na, D, D), D),
        "conv_w_pw1": nrm(ks[6], (nc, D, 2 * D), D),
        "conv_b_pw1": small(ks[7], (nc, 2 * D), 0.02),
        "conv_w_dw": nrm(ks[8], (nc, CONV_WIDTH, D), CONV_WIDTH),
        "conv_b_dw": small(ks[9], (nc, D), 0.02),
        "conv_ln_g": 1.0 + small(ks[10], (nc, D), 0.1),
        "conv_ln_b": small(ks[11], (nc, D), 0.02),
        "conv_w_pw2": nrm(ks[12], (nc, D, D), D),
        "conv_b_pw2": small(ks[13], (nc, D), 0.02),
        "ffn_w_up": nrm(ks[14], (DEPTH, D, 2 * F), D),
        "ffn_w_dw": nrm(ks[15], (DEPTH, FFN_CONV_WIDTH, 2 * F), FFN_CONV_WIDTH),
        "ffn_b_dw": small(ks[16], (DEPTH, 2 * F), 0.02),
        "ffn_w_down": nrm(ks[17], (DEPTH, F, D), F),
    }


def reference(x, norm_mix_g, norm_ffn_g, final_norm_g, attn_w_qkv, attn_w_o,
              conv_w_pw1, conv_b_pw1, conv_w_dw, conv_b_dw, conv_ln_g, conv_ln_b,
              conv_w_pw2, conv_b_pw2, ffn_w_up, ffn_w_dw, ffn_b_dw, ffn_w_down):
    h = x
    for i in range(DEPTH):
        j = i // N_MIXERS
        hn = rmsnorm(h, norm_mix_g[i])
        if i % N_MIXERS == 0:
            h = h + moba_attention(hn, attn_w_qkv[j], attn_w_o[j])
        else:
            h = h + conformer_conv(hn, conv_w_pw1[j], conv_b_pw1[j], conv_w_dw[j], conv_b_dw[j],
                                   conv_ln_g[j], conv_ln_b[j], conv_w_pw2[j], conv_b_pw2[j])
        h = h + conv_ffn(rmsnorm(h, norm_ffn_g[i]), ffn_w_up[i], ffn_w_dw[i], ffn_b_dw[i], ffn_w_down[i])
    return rmsnorm(h, final_norm_g)
```

```python
import functools

import jax
import jax.numpy as jnp
from jax import lax
from jax.experimental import pallas as pl
from jax.experimental.pallas import tpu as pltpu

N_HEADS = 16
HEAD_DIM = 64
ROPE_THETA = 10000.0
MOBA_BLOCK = 256
MOBA_TOPK = 3
NORM_EPS = 1e-6
LN_EPS = 1e-5

LANES = 128
SUBLANES = 8
HEADS_PER_LANE_TILE = LANES // HEAD_DIM
VMEM_LIMIT_BYTES = 56 * 1024 * 1024

ROW_TILE = 512
FFN_CHUNKS = (1024, 1024, 768)
CONV_ROW_CHUNK = 64
CONV_HALO = 32
NEG_BIG = -1e30

_NT = (((1,), (1,)), ((), ()))
_TN = (((0,), (0,)), ((), ()))


def _dot(a, b):
    return jnp.dot(a, b, preferred_element_type=jnp.float32)


def _rmsnorm(x, g):
    return x * lax.rsqrt(jnp.mean(x * x, axis=-1, keepdims=True) + NORM_EPS) * g


def _const_spec(shape):
    return pl.BlockSpec(shape, lambda *_: (0,) * len(shape), pipeline_mode=pl.Buffered(1))


def _params(n_axes):
    return pltpu.CompilerParams(dimension_semantics=("arbitrary",) * n_axes,
                                vmem_limit_bytes=VMEM_LIMIT_BYTES)


def _qkv_kernel(h_ref, g_ref, w_ref, cos_ref, sin_ref, q_ref, k_ref, v_ref, km_ref):
    tm, d = h_ref.shape
    hn = _rmsnorm(h_ref[...], g_ref[...]).astype(jnp.bfloat16)
    cos = cos_ref[...]
    sin = sin_ref[...]
    lane = lax.broadcasted_iota(jnp.int32, cos.shape, 1)
    first_half = (lane & (HEAD_DIM // 2)) == 0

    def rope(t):
        rot = jnp.where(first_half, pltpu.roll(t, LANES - HEAD_DIM // 2, 1), pltpu.roll(t, HEAD_DIM // 2, 1))
        return t * cos + rot * sin

    q = _dot(hn, w_ref[:, 0:d])
    for c in range(d // LANES):
        sl = slice(c * LANES, (c + 1) * LANES)
        q_ref[:, sl] = (rope(q[:, sl]) * (HEAD_DIM ** -0.5)).astype(q_ref.dtype)
    k = _dot(hn, w_ref[:, d:2 * d])
    for c in range(d // LANES):
        sl = slice(c * LANES, (c + 1) * LANES)
        kr = rope(k[:, sl])
        k_ref[:, sl] = kr.astype(k_ref.dtype)
        km_ref[0, :, sl] = jnp.mean(kr.reshape(tm // MOBA_BLOCK, MOBA_BLOCK, LANES), axis=1)
    v_ref[...] = _dot(hn, w_ref[:, 2 * d:3 * d]).astype(v_ref.dtype)


def _qkv_call(h, g, w_qkv, cos, sin, seq):
    n, d = h.shape
    tm = ROW_TILE
    steps_per_seq = seq // tm
    row = lambda i: (i, 0)
    out_bf16 = jax.ShapeDtypeStruct((n, d), jnp.bfloat16)
    return pl.pallas_call(
        _qkv_kernel,
        grid=(n // tm,),
        in_specs=[pl.BlockSpec((tm, d), row), _const_spec((1, d)), _const_spec((d, 3 * d)),
                  pl.BlockSpec((tm, LANES), lambda i: (i % steps_per_seq, 0)),
                  pl.BlockSpec((tm, LANES), lambda i: (i % steps_per_seq, 0))],
        out_specs=[pl.BlockSpec((tm, d), row)] * 3
        + [pl.BlockSpec((1, tm // MOBA_BLOCK, d), lambda i: (i, 0, 0))],
        out_shape=[out_bf16] * 3 + [jax.ShapeDtypeStruct((n // tm, tm // MOBA_BLOCK, d), jnp.float32)],
        compiler_params=_params(1),
        name="qkv_rope",
    )(h, g, w_qkv, cos, sin)


def _attn_kernel(q_ref, k_ref, v_ref, km_ref, o_ref):
    seq = q_ref.shape[0]
    blk = MOBA_BLOCK
    nb = seq // blk
    q_lane = lax.broadcasted_iota(jnp.int32, (blk, LANES), 1)
    causal = (lax.broadcasted_iota(jnp.int32, (blk, blk), 0) <= lax.broadcasted_iota(jnp.int32, (blk, blk), 1))
    out_row = lax.broadcasted_iota(jnp.int32, (LANES, blk), 0)
    km = km_ref[0]
    km_hi = km.astype(jnp.bfloat16)
    km_lo = (km - km_hi.astype(jnp.float32)).astype(jnp.bfloat16)

    for i in range(nb):
        q = q_ref[i * blk:(i + 1) * blk, :]
        n_keys = (i + 1) * blk
        head_out = []
        for hh in range(HEADS_PER_LANE_TILE):
            in_head = (q_lane >= hh * HEAD_DIM) & (q_lane < (hh + 1) * HEAD_DIM)
            qm = jnp.where(in_head, q, jnp.zeros_like(q))
            s_t = lax.dot_general(k_ref[0:n_keys, :], qm, _NT, preferred_element_type=jnp.float32)
            pieces = []
            if i > MOBA_TOPK:
                gate = (lax.dot_general(km_hi, qm, _NT, preferred_element_type=jnp.float32)
                        + lax.dot_general(km_lo, qm, _NT, preferred_element_type=jnp.float32))
                rows = [gate[j:j + 1, :] for j in range(i)]
                for j in range(i):
                    rank = jnp.zeros((1, blk), jnp.float32)
                    for j2 in range(i):
                        if j2 == j:
                            continue
                        beats = (rows[j2] >= rows[j]) if j2 < j else (rows[j2] > rows[j])
                        rank = rank + jnp.where(beats, 1.0, 0.0)
                    chosen = rank < MOBA_TOPK
                    pieces.append(jnp.where(chosen, s_t[j * blk:(j + 1) * blk, :], NEG_BIG))
            else:
                for j in range(i):
                    pieces.append(s_t[j * blk:(j + 1) * blk, :])
            pieces.append(jnp.where(causal, s_t[i * blk:, :], NEG_BIG))
            s_t = jnp.concatenate(pieces, axis=0) if len(pieces) > 1 else pieces[0]
            m = jnp.max(s_t, axis=0, keepdims=True)
            p = jnp.exp(s_t - m)
            l = jnp.sum(p, axis=0, keepdims=True)
            o_t = lax.dot_general(v_ref[0:n_keys, :], p.astype(jnp.bfloat16), _TN,
                                  preferred_element_type=jnp.float32)
            head_out.append(o_t * (1.0 / l))
        o_t = jnp.where(out_row < HEAD_DIM, head_out[0], head_out[1])
        o_ref[i * blk:(i + 1) * blk, :] = o_t.T.astype(o_ref.dtype)


def _attn_call(q, k, v, kmean, batch, seq):
    n, d = q.shape
    nb = seq // MOBA_BLOCK
    tile = lambda b, t: (b, t)
    return pl.pallas_call(
        _attn_kernel,
        grid=(batch, d // LANES),
        in_specs=[pl.BlockSpec((seq, LANES), tile)] * 3
        + [pl.BlockSpec((1, nb, LANES), lambda b, t: (b, 0, t))],
        out_specs=pl.BlockSpec((seq, LANES), tile),
        out_shape=jax.ShapeDtypeStruct((n, d), jnp.bfloat16),
        compiler_params=_params(2),
        name="moba_attn",
    )(q, k, v, kmean)


def _conv_kernel(h_ref, g_ref, w1_ref, b1_ref, wdw_ref, bdw_ref, lng_ref, lnb_ref, m_ref,
                 gbuf_ref, cbuf_ref, *, steps_per_seq):
    tm, d = h_ref.shape
    width = wdw_ref.shape[0]
    first_tap = CONV_HALO - (width - 1)
    seq_start = (pl.program_id(0) % steps_per_seq) == 0
    hn = _rmsnorm(h_ref[...], g_ref[...]).astype(jnp.bfloat16)
    a = _dot(hn, w1_ref[:, 0:d]) + b1_ref[:, 0:d]
    gate = _dot(hn, w1_ref[:, d:2 * d]) + b1_ref[:, d:2 * d]
    glu = a * jax.nn.sigmoid(gate)

    @pl.when(seq_start)
    def _():
        gbuf_ref[0:CONV_HALO, :] = jnp.zeros((CONV_HALO, d), jnp.float32)

    @pl.when(jnp.logical_not(seq_start))
    def _():
        gbuf_ref[0:CONV_HALO, :] = gbuf_ref[tm:tm + CONV_HALO, :]

    gbuf_ref[CONV_HALO:CONV_HALO + tm, :] = glu

    rc = CONV_ROW_CHUNK
    for lc in range(d // LANES):
        ls = slice(lc * LANES, (lc + 1) * LANES)

        def body(step, carry, ls=ls):
            r0 = pl.multiple_of(step * rc, rc)
            y = jnp.broadcast_to(bdw_ref[:, ls], (rc, LANES))
            for r in range(SUBLANES):
                rows = rc + (SUBLANES if r else 0)
                z = None
                for a_ in range(CONV_HALO // SUBLANES + 1):
                    o = SUBLANES * a_ + r
                    if o < first_tap or o > CONV_HALO:
                        continue
                    t = gbuf_ref[pl.ds(r0 + SUBLANES * a_, rows), ls] * wdw_ref[o - first_tap:o - first_tap + 1, ls]
                    z = t if z is None else z + t
                y = y + z[r:r + rc, :]
            cbuf_ref[pl.ds(r0, rc), ls] = y
            return carry

        lax.fori_loop(0, tm // rc, body, 0)

    c = cbuf_ref[...]
    mu = jnp.mean(c, axis=-1, keepdims=True)
    cc = c - mu
    var = jnp.mean(cc * cc, axis=-1, keepdims=True)
    y = cc * lax.rsqrt(var + LN_EPS) * lng_ref[...] + lnb_ref[...]
    m_ref[...] = (y * jax.nn.sigmoid(y)).astype(m_ref.dtype)


def _conv_call(h, g, w1, b1, wdw, bdw, lng, lnb, seq):
    n, d = h.shape
    tm = ROW_TILE
    width = wdw.shape[0]
    assert width - 1 <= CONV_HALO and CONV_HALO % SUBLANES == 0
    row = lambda i: (i, 0)
    return pl.pallas_call(
        functools.partial(_conv_kernel, steps_per_seq=seq // tm),
        grid=(n // tm,),
        in_specs=[pl.BlockSpec((tm, d), row), _const_spec((1, d)), _const_spec((d, 2 * d)),
                  _const_spec((1, 2 * d)), _const_spec((width, d)), _const_spec((1, d)),
                  _const_spec((1, d)), _const_spec((1, d))],
        out_specs=pl.BlockSpec((tm, d), row),
        out_shape=jax.ShapeDtypeStruct((n, d), jnp.bfloat16),
        scratch_shapes=[pltpu.VMEM((tm + CONV_HALO, d), jnp.float32), pltpu.VMEM((tm, d), jnp.float32)],
        compiler_params=_params(1),
        name="conformer_conv",
    )(h, g, w1, b1, wdw, bdw, lng, lnb)


def _ffn_kernel(h_ref, m_ref, wm_ref, bm_ref, g_ref, wup_ref, wdw_ref, bdw_ref, wdn_ref, gfin_ref, o_ref,
                carry_ref, *, steps_per_seq, final_norm):
    tm, d = h_ref.shape
    f = wdn_ref.shape[0]
    width = wdw_ref.shape[0]
    seq_start = (pl.program_id(0) % steps_per_seq) == 0
    h_mid = h_ref[...] + _dot(m_ref[...], wm_ref[...]) + bm_ref[...]
    hn = _rmsnorm(h_mid, g_ref[...]).astype(jnp.bfloat16)
    acc = h_mid
    off = 0
    for fn in FFN_CHUNKS:
        halves = []
        for part in range(2):
            cs = part * f + off
            u = _dot(hn, wup_ref[:, cs:cs + fn])
            prev = jnp.where(seq_start, 0.0, carry_ref[:, cs:cs + fn])
            carry_ref[:, cs:cs + fn] = u[tm - SUBLANES:, :]
            ext = jnp.concatenate([prev, u], axis=0)
            y = bdw_ref[:, cs:cs + fn] + u * wdw_ref[width - 1:width, cs:cs + fn]
            for j in range(width - 1):
                shift = width - 1 - j
                y = y + ext[SUBLANES - shift:SUBLANES - shift + tm, :] * wdw_ref[j:j + 1, cs:cs + fn]
            halves.append(y)
        act = (halves[0] * jax.nn.sigmoid(halves[0]) * halves[1]).astype(jnp.bfloat16)
        acc = acc + _dot(act, wdn_ref[off:off + fn, :])
        off += fn
    if final_norm:
        acc = _rmsnorm(acc, gfin_ref[...])
    o_ref[...] = acc


def _ffn_call(h, m, wm, bm, g, wup, wdw, bdw, wdn, gfin, seq, final_norm):
    n, d = h.shape
    f = wdn.shape[0]
    tm = ROW_TILE
    width = wdw.shape[0]
    assert sum(FFN_CHUNKS) == f and width - 1 <= SUBLANES
    row = lambda i: (i, 0)
    return pl.pallas_call(
        functools.partial(_ffn_kernel, steps_per_seq=seq // tm, final_norm=final_norm),
        grid=(n // tm,),
        in_specs=[pl.BlockSpec((tm, d), row), pl.BlockSpec((tm, d), row), _const_spec((d, d)),
                  _const_spec((1, d)), _const_spec((1, d)), _const_spec((d, 2 * f)),
                  _const_spec((width, 2 * f)), _const_spec((1, 2 * f)), _const_spec((f, d)),
                  _const_spec((1, d))],
        out_specs=pl.BlockSpec((tm, d), row),
        out_shape=jax.ShapeDtypeStruct((n, d), jnp.float32),
        scratch_shapes=[pltpu.VMEM((SUBLANES, 2 * f), jnp.float32)],
        compiler_params=_params(1),
        name="mix_proj_convffn",
    )(h, m, wm, bm, g, wup, wdw, bdw, wdn, gfin)


def _rope_tables(seq):
    pos = jnp.arange(seq, dtype=jnp.float32)
    inv_freq = 1.0 / (ROPE_THETA ** (jnp.arange(0, HEAD_DIM, 2, dtype=jnp.float32) / HEAD_DIM))
    ang = pos[:, None] * inv_freq[None, :]
    cos, sin = jnp.cos(ang), jnp.sin(ang)
    cos_t = jnp.tile(jnp.concatenate([cos, cos], axis=-1), (1, HEADS_PER_LANE_TILE))
    sin_t = jnp.tile(jnp.concatenate([-sin, sin], axis=-1), (1, HEADS_PER_LANE_TILE))
    return cos_t, sin_t


def kernel(x, norm_mix_g, norm_ffn_g, final_norm_g, attn_w_qkv, attn_w_o, conv_w_pw1, conv_b_pw1, conv_w_dw, conv_b_dw, conv_ln_g, conv_ln_b, conv_w_pw2, conv_b_pw2, ffn_w_up, ffn_w_dw, ffn_b_dw, ffn_w_down):
    batch, seq, d = x.shape
    depth = norm_mix_g.shape[0]
    bf16 = jnp.bfloat16
    cos_t, sin_t = _rope_tables(seq)
    zero_bias = jnp.zeros((1, d), jnp.float32)
    h = x.reshape(batch * seq, d)
    for i in range(depth):
        j = i // 2
        g_mix = norm_mix_g[i][None, :]
        if i % 2 == 0:
            q, k, v, kmean = _qkv_call(h, g_mix, attn_w_qkv[j].astype(bf16), cos_t, sin_t, seq)
            kmean = kmean.reshape(batch, seq // MOBA_BLOCK, d)
            m = _attn_call(q, k, v, kmean, batch, seq)
            wm, bm = attn_w_o[j].astype(bf16), zero_bias
        else:
            m = _conv_call(h, g_mix, conv_w_pw1[j].astype(bf16), conv_b_pw1[j][None, :], conv_w_dw[j],
                           conv_b_dw[j][None, :], conv_ln_g[j][None, :], conv_ln_b[j][None, :], seq)
            wm, bm = conv_w_pw2[j].astype(bf16), conv_b_pw2[j][None, :]
        h = _ffn_call(h, m, wm, bm, norm_ffn_g[i][None, :], ffn_w_up[i].astype(bf16), ffn_w_dw[i],
                      ffn_b_dw[i][None, :], ffn_w_down[i].astype(bf16), final_norm_g[None, :], seq,
                      final_norm=(i == depth - 1))
    return h.reshape(batch, seq, d)
```

```python
import functools

import jax
import jax.numpy as jnp
from jax import lax
from jax.experimental import pallas as pl
from jax.experimental.pallas import tpu as pltpu

N_HEADS = 16
HEAD_DIM = 64
ROPE_THETA = 10000.0
MOBA_BLOCK = 256
MOBA_TOPK = 3
NORM_EPS = 1e-6
LN_EPS = 1e-5

LANES = 128
SUBLANES = 8
HEADS_PER_LANE_TILE = LANES // HEAD_DIM
VMEM_LIMIT_BYTES = 56 * 1024 * 1024

ROW_TILE = 512
FFN_CHUNKS = (1024, 1024, 768)
CONV_ROW_CHUNK = 64
CONV_HALO = 32
NEG_BIG = -1e30
Q_SCALE = HEAD_DIM ** -0.5 * 1.4426950408889634

_NT = (((1,), (1,)), ((), ()))
_TN = (((0,), (0,)), ((), ()))


def _dot(a, b):
    return jnp.dot(a, b, preferred_element_type=jnp.float32)


def _rmsnorm(x, g):
    return x * lax.rsqrt(jnp.mean(x * x, axis=-1, keepdims=True) + NORM_EPS) * g


def _const_spec(shape):
    return pl.BlockSpec(shape, lambda *_: (0,) * len(shape), pipeline_mode=pl.Buffered(1))


def _layer_spec(layer, shape):
    return pl.BlockSpec((None,) + shape, lambda *_: (layer,) + (0,) * len(shape), pipeline_mode=pl.Buffered(1))


def _params(n_axes):
    return pltpu.CompilerParams(dimension_semantics=("arbitrary",) * n_axes,
                                vmem_limit_bytes=VMEM_LIMIT_BYTES)


def _qkv_kernel(h_ref, g_ref, w_ref, cos_ref, sin_ref, q_ref, k_ref, v_ref, km_ref):
    tm, d = h_ref.shape
    hn = _rmsnorm(h_ref[...], g_ref[...]).astype(jnp.bfloat16)
    cos = cos_ref[...]
    sin = sin_ref[...]
    lane = lax.broadcasted_iota(jnp.int32, cos.shape, 1)
    first_half = (lane & (HEAD_DIM // 2)) == 0

    def rope(t):
        rot = jnp.where(first_half, pltpu.roll(t, LANES - HEAD_DIM // 2, 1), pltpu.roll(t, HEAD_DIM // 2, 1))
        return t * cos + rot * sin

    q = _dot(hn, w_ref[:, 0:d])
    for c in range(d // LANES):
        sl = slice(c * LANES, (c + 1) * LANES)
        q_ref[:, sl] = (rope(q[:, sl]) * Q_SCALE).astype(q_ref.dtype)
    k = _dot(hn, w_ref[:, d:2 * d])
    for c in range(d // LANES):
        sl = slice(c * LANES, (c + 1) * LANES)
        kr = rope(k[:, sl])
        k_ref[:, sl] = kr.astype(k_ref.dtype)
        km_ref[0, :, sl] = jnp.mean(kr.reshape(tm // MOBA_BLOCK, MOBA_BLOCK, LANES), axis=1)
    v_ref[...] = _dot(hn, w_ref[:, 2 * d:3 * d]).astype(v_ref.dtype)


def _qkv_call(h, g, w_qkv, cos, sin, seq, layer, mixer):
    n, d = h.shape
    tm = ROW_TILE
    steps_per_seq = seq // tm
    row = lambda i: (i, 0)
    out_bf16 = jax.ShapeDtypeStruct((n, d), jnp.bfloat16)
    return pl.pallas_call(
        _qkv_kernel,
        grid=(n // tm,),
        in_specs=[pl.BlockSpec((tm, d), row), _layer_spec(layer, (1, d)), _layer_spec(mixer, (d, 3 * d)),
                  pl.BlockSpec((tm, LANES), lambda i: (i % steps_per_seq, 0)),
                  pl.BlockSpec((tm, LANES), lambda i: (i % steps_per_seq, 0))],
        out_specs=[pl.BlockSpec((tm, d), row)] * 3
        + [pl.BlockSpec((1, tm // MOBA_BLOCK, d), lambda i: (i, 0, 0))],
        out_shape=[out_bf16] * 3 + [jax.ShapeDtypeStruct((n // tm, tm // MOBA_BLOCK, d), jnp.float32)],
        compiler_params=_params(1),
        name="qkv_rope",
    )(h, g, w_qkv, cos, sin)


def _attn_kernel(q_ref, k_ref, v_ref, km_ref, o_ref, s_buf, p_buf):
    seq = q_ref.shape[0]
    blk = MOBA_BLOCK
    nb = seq // blk
    q_lane = lax.broadcasted_iota(jnp.int32, (blk, LANES), 1)
    causal = (lax.broadcasted_iota(jnp.int32, (blk, blk), 0) <= lax.broadcasted_iota(jnp.int32, (blk, blk), 1))
    out_row = lax.broadcasted_iota(jnp.int32, (LANES, blk), 0)
    km = km_ref[0]
    km_hi = km.astype(jnp.bfloat16)
    km_hl = jnp.concatenate([km_hi, (km - km_hi.astype(jnp.float32)).astype(jnp.bfloat16)], axis=0)

    def scores(i, hh, slot):
        n_keys = (i + 1) * blk
        q = q_ref[i * blk:(i + 1) * blk, :]
        in_head = (q_lane >= hh * HEAD_DIM) & (q_lane < (hh + 1) * HEAD_DIM)
        qm = jnp.where(in_head, q, jnp.zeros_like(q))
        s_t = lax.dot_general(k_ref[0:n_keys, :], qm, _NT, preferred_element_type=jnp.float32)
        pieces = []
        if i > MOBA_TOPK:
            gate = lax.dot_general(km_hl, qm, _NT, preferred_element_type=jnp.float32)
            gate = gate[0:nb, :] + gate[nb:2 * nb, :]
            rows = [gate[j:j + 1, :] for j in range(i)]
            for j in range(i):
                rank = jnp.zeros((1, blk), jnp.float32)
                for j2 in range(i):
                    if j2 == j:
                        continue
                    beats = (rows[j2] >= rows[j]) if j2 < j else (rows[j2] > rows[j])
                    rank = rank + jnp.where(beats, 1.0, 0.0)
                pieces.append(jnp.where(rank < MOBA_TOPK, s_t[j * blk:(j + 1) * blk, :], NEG_BIG))
        else:
            for j in range(i):
                pieces.append(s_t[j * blk:(j + 1) * blk, :])
        pieces.append(jnp.where(causal, s_t[i * blk:, :], NEG_BIG))
        m = None
        for j, piece in enumerate(pieces):
            s_buf[slot, j * blk:(j + 1) * blk, :] = piece
            pm = jnp.max(piece, axis=0, keepdims=True)
            m = pm if m is None else jnp.maximum(m, pm)
        return m

    def probs(i, slot, m):
        l = None
        for j in range(i + 1):
            p = jnp.exp2(s_buf[slot, j * blk:(j + 1) * blk, :] - m)
            p_buf[slot, j * blk:(j + 1) * blk, :] = p.astype(p_buf.dtype)
            pl_ = jnp.sum(p, axis=0, keepdims=True)
            l = pl_ if l is None else l + pl_
        return l

    def weighted_values(i, slot, l):
        n_keys = (i + 1) * blk
        o_t = lax.dot_general(v_ref[0:n_keys, :], p_buf[slot, 0:n_keys, :], _TN,
                              preferred_element_type=jnp.float32)
        return o_t * (1.0 / l)

    units = [(i, hh) for i in range(nb) for hh in range(HEADS_PER_LANE_TILE)]
    m_next = scores(*units[0], 0)
    head_out = []
    for n, (i, hh) in enumerate(units):
        m = m_next
        if n + 1 < len(units):
            m_next = scores(*units[n + 1], (n + 1) % 2)
        l = probs(i, n % 2, m)
        head_out.append(weighted_values(i, n % 2, l))
        if hh == HEADS_PER_LANE_TILE - 1:
            o_t = jnp.where(out_row < HEAD_DIM, head_out[0], head_out[1])
            o_ref[i * blk:(i + 1) * blk, :] = o_t.T.astype(o_ref.dtype)
            head_out = []


def _attn_call(q, k, v, kmean, batch, seq):
    n, d = q.shape
    nb = seq // MOBA_BLOCK
    assert HEADS_PER_LANE_TILE == 2
    tile = lambda b, t: (b, t)
    return pl.pallas_call(
        _attn_kernel,
        grid=(batch, d // LANES),
        in_specs=[pl.BlockSpec((seq, LANES), tile)] * 3
        + [pl.BlockSpec((1, nb, LANES), lambda b, t: (b, 0, t))],
        out_specs=pl.BlockSpec((seq, LANES), tile),
        out_shape=jax.ShapeDtypeStruct((n, d), jnp.bfloat16),
        scratch_shapes=[pltpu.VMEM((2, seq, MOBA_BLOCK), jnp.float32),
                        pltpu.VMEM((2, seq, MOBA_BLOCK), jnp.bfloat16)],
        compiler_params=_params(2),
        name="moba_attn",
    )(q, k, v, kmean)


def _conv_kernel(h_ref, g_ref, w1_ref, b1_ref, wdw_ref, bdw_ref, lng_ref, lnb_ref, m_ref,
                 gbuf_ref, cbuf_ref, *, steps_per_seq):
    tm, d = h_ref.shape
    width = wdw_ref.shape[0]
    first_tap = CONV_HALO - (width - 1)
    seq_start = (pl.program_id(0) % steps_per_seq) == 0
    hn = _rmsnorm(h_ref[...], g_ref[...]).astype(jnp.bfloat16)
    a = _dot(hn, w1_ref[:, 0:d]) + b1_ref[:, 0:d]
    gate = _dot(hn, w1_ref[:, d:2 * d]) + b1_ref[:, d:2 * d]
    glu = a * jax.nn.sigmoid(gate)

    @pl.when(seq_start)
    def _():
        gbuf_ref[0:CONV_HALO, :] = jnp.zeros((CONV_HALO, d), jnp.float32)

    @pl.when(jnp.logical_not(seq_start))
    def _():
        gbuf_ref[0:CONV_HALO, :] = gbuf_ref[tm:tm + CONV_HALO, :]

    gbuf_ref[CONV_HALO:CONV_HALO + tm, :] = glu

    rc = CONV_ROW_CHUNK
    for lc in range(d // LANES):
        ls = slice(lc * LANES, (lc + 1) * LANES)

        def body(step, carry, ls=ls):
            r0 = pl.multiple_of(step * rc, rc)
            y = jnp.broadcast_to(bdw_ref[:, ls], (rc, LANES))
            for r in range(SUBLANES):
                rows = rc + (SUBLANES if r else 0)
                z = None
                for a_ in range(CONV_HALO // SUBLANES + 1):
                    o = SUBLANES * a_ + r
                    if o < first_tap or o > CONV_HALO:
                        continue
                    t = gbuf_ref[pl.ds(r0 + SUBLANES * a_, rows), ls] * wdw_ref[o - first_tap:o - first_tap + 1, ls]
                    z = t if z is None else z + t
                y = y + z[r:r + rc, :]
            cbuf_ref[pl.ds(r0, rc), ls] = y
            return carry

        lax.fori_loop(0, tm // rc, body, 0)

    c = cbuf_ref[...]
    mu = jnp.mean(c, axis=-1, keepdims=True)
    cc = c - mu
    var = jnp.mean(cc * cc, axis=-1, keepdims=True)
    y = cc * lax.rsqrt(var + LN_EPS) * lng_ref[...] + lnb_ref[...]
    m_ref[...] = (y * jax.nn.sigmoid(y)).astype(m_ref.dtype)


def _conv_call(h, g, w1, b1, wdw, bdw, lng, lnb, seq, layer, mixer):
    n, d = h.shape
    tm = ROW_TILE
    width = wdw.shape[1]
    assert width - 1 <= CONV_HALO and CONV_HALO % SUBLANES == 0
    row = lambda i: (i, 0)
    return pl.pallas_call(
        functools.partial(_conv_kernel, steps_per_seq=seq // tm),
        grid=(n // tm,),
        in_specs=[pl.BlockSpec((tm, d), row), _layer_spec(layer, (1, d)), _layer_spec(mixer, (d, 2 * d)),
                  _layer_spec(mixer, (1, 2 * d)), _layer_spec(mixer, (width, d)), _layer_spec(mixer, (1, d)),
                  _layer_spec(mixer, (1, d)), _layer_spec(mixer, (1, d))],
        out_specs=pl.BlockSpec((tm, d), row),
        out_shape=jax.ShapeDtypeStruct((n, d), jnp.bfloat16),
        scratch_shapes=[pltpu.VMEM((tm + CONV_HALO, d), jnp.float32), pltpu.VMEM((tm, d), jnp.float32)],
        compiler_params=_params(1),
        name="conformer_conv",
    )(h, g, w1, b1, wdw, bdw, lng, lnb)


def _ffn_kernel(h_ref, m_ref, wm_ref, bm_ref, g_ref, wup_ref, wdw_ref, bdw_ref, wdn_ref, gfin_ref, o_ref,
                carry_ref, *, steps_per_seq, final_norm):
    tm, d = h_ref.shape
    f = wdn_ref.shape[0]
    width = wdw_ref.shape[0]
    seq_start = (pl.program_id(0) % steps_per_seq) == 0
    h_mid = h_ref[...] + _dot(m_ref[...], wm_ref[...]) + bm_ref[...]
    hn = _rmsnorm(h_mid, g_ref[...]).astype(jnp.bfloat16)
    acc = h_mid
    off = 0
    for fn in FFN_CHUNKS:
        halves = []
        for part in range(2):
            cs = part * f + off
            u = _dot(hn, wup_ref[:, cs:cs + fn])
            prev = jnp.where(seq_start, 0.0, carry_ref[:, cs:cs + fn])
            carry_ref[:, cs:cs + fn] = u[tm - SUBLANES:, :]
            ext = jnp.concatenate([prev, u], axis=0)
            y = bdw_ref[:, cs:cs + fn] + u * wdw_ref[width - 1:width, cs:cs + fn]
            for j in range(width - 1):
                shift = width - 1 - j
                y = y + ext[SUBLANES - shift:SUBLANES - shift + tm, :] * wdw_ref[j:j + 1, cs:cs + fn]
            halves.append(y)
        act = (halves[0] * jax.nn.sigmoid(halves[0]) * halves[1]).astype(jnp.bfloat16)
        acc = acc + _dot(act, wdn_ref[off:off + fn, :])
        off += fn
    if final_norm:
        acc = _rmsnorm(acc, gfin_ref[...])
    o_ref[...] = acc


def _ffn_call(h, m, wm, bm, g, wup, wdw, bdw, wdn, gfin, seq, layer, mixer, final_norm):
    n, d = h.shape
    f = wdn.shape[1]
    tm = ROW_TILE
    width = wdw.shape[1]
    assert sum(FFN_CHUNKS) == f and width - 1 <= SUBLANES
    row = lambda i: (i, 0)
    return pl.pallas_call(
        functools.partial(_ffn_kernel, steps_per_seq=seq // tm, final_norm=final_norm),
        grid=(n // tm,),
        in_specs=[pl.BlockSpec((tm, d), row), pl.BlockSpec((tm, d), row), _layer_spec(mixer, (d, d)),
                  _layer_spec(mixer, (1, d)), _layer_spec(layer, (1, d)), _layer_spec(layer, (d, 2 * f)),
                  _layer_spec(layer, (width, 2 * f)), _layer_spec(layer, (1, 2 * f)), _layer_spec(layer, (f, d)),
                  _const_spec((1, d))],
        out_specs=pl.BlockSpec((tm, d), row),
        out_shape=jax.ShapeDtypeStruct((n, d), jnp.float32),
        scratch_shapes=[pltpu.VMEM((SUBLANES, 2 * f), jnp.float32)],
        compiler_params=_params(1),
        name="mix_proj_convffn",
    )(h, m, wm, bm, g, wup, wdw, bdw, wdn, gfin)


def _rope_tables(seq):
    pos = jnp.arange(seq, dtype=jnp.float32)
    inv_freq = 1.0 / (ROPE_THETA ** (jnp.arange(0, HEAD_DIM, 2, dtype=jnp.float32) / HEAD_DIM))
    ang = pos[:, None] * inv_freq[None, :]
    cos, sin = jnp.cos(ang), jnp.sin(ang)
    cos_t = jnp.tile(jnp.concatenate([cos, cos], axis=-1), (1, HEADS_PER_LANE_TILE))
    sin_t = jnp.tile(jnp.concatenate([-sin, sin], axis=-1), (1, HEADS_PER_LANE_TILE))
    return cos_t, sin_t


def kernel(x, norm_mix_g, norm_ffn_g, final_norm_g, attn_w_qkv, attn_w_o, conv_w_pw1, conv_b_pw1, conv_w_dw, conv_b_dw, conv_ln_g, conv_ln_b, conv_w_pw2, conv_b_pw2, ffn_w_up, ffn_w_dw, ffn_b_dw, ffn_w_down):
    batch, seq, d = x.shape
    depth = norm_mix_g.shape[0]
    bf16 = jnp.bfloat16
    cos_t, sin_t = _rope_tables(seq)
    row_vec = lambda a: a[:, None, :]
    g_mix, g_ffn = row_vec(norm_mix_g), row_vec(norm_ffn_g)
    w_qkv, w_o = attn_w_qkv.astype(bf16), attn_w_o.astype(bf16)
    b_o = jnp.zeros((attn_w_o.shape[0], 1, d), jnp.float32)
    w_pw1, w_pw2 = conv_w_pw1.astype(bf16), conv_w_pw2.astype(bf16)
    w_up, w_down = ffn_w_up.astype(bf16), ffn_w_down.astype(bf16)
    h = x.reshape(batch * seq, d)
    for i in range(depth):
        j = i // 2
        if i % 2 == 0:
            q, k, v, kmean = _qkv_call(h, g_mix, w_qkv, cos_t, sin_t, seq, i, j)
            kmean = kmean.reshape(batch, seq // MOBA_BLOCK, d)
            m = _attn_call(q, k, v, kmean, batch, seq)
            wm, bm = w_o, b_o
        else:
            m = _conv_call(h, g_mix, w_pw1, row_vec(conv_b_pw1), conv_w_dw, row_vec(conv_b_dw),
                           row_vec(conv_ln_g), row_vec(conv_ln_b), seq, i, j)
            wm, bm = w_pw2, row_vec(conv_b_pw2)
        h = _ffn_call(h, m, wm, bm, g_ffn, w_up, ffn_w_dw, row_vec(ffn_b_dw), w_down, final_norm_g[None, :],
                      seq, i, j, final_norm=(i == depth - 1))
    return h.reshape(batch, seq, d)
```

```python
import functools

import jax
import jax.numpy as jnp
from jax import lax
from jax.experimental import pallas as pl
from jax.experimental.pallas import tpu as pltpu

N_HEADS = 16
HEAD_DIM = 64
ROPE_THETA = 10000.0
MOBA_BLOCK = 256
MOBA_TOPK = 3
NORM_EPS = 1e-6
LN_EPS = 1e-5

LANES = 128
SUBLANES = 8
HEADS_PER_LANE_TILE = LANES // HEAD_DIM
VMEM_LIMIT_BYTES = 56 * 1024 * 1024

ROW_TILE = 512
FFN_CHUNKS = (1024, 1024, 768)
CONV_ROW_CHUNK = 64
CONV_HALO = 32
ATTN_SLOTS = 3
ATTN_FLAGS = None
ONES_ROWS = 16
NEG_BIG = -1e30
Q_SCALE = HEAD_DIM ** -0.5 * 1.4426950408889634

_NT = (((1,), (1,)), ((), ()))
_TN = (((0,), (0,)), ((), ()))


def _dot(a, b):
    return jnp.dot(a, b, preferred_element_type=jnp.float32)


def _rmsnorm(x, g):
    return x * lax.rsqrt(jnp.mean(x * x, axis=-1, keepdims=True) + NORM_EPS) * g


def _const_spec(shape):
    return pl.BlockSpec(shape, lambda *_: (0,) * len(shape), pipeline_mode=pl.Buffered(1))


def _layer_spec(layer, shape):
    return pl.BlockSpec((None,) + shape, lambda *_: (layer,) + (0,) * len(shape), pipeline_mode=pl.Buffered(1))


def _params(n_axes, flags=None):
    return pltpu.CompilerParams(dimension_semantics=("arbitrary",) * n_axes,
                                vmem_limit_bytes=VMEM_LIMIT_BYTES, flags=flags)


def _qkv_kernel(h_ref, g_ref, w_ref, cos_ref, sin_ref, q_ref, k_ref, v_ref, km_ref):
    tm, d = h_ref.shape
    hn = _rmsnorm(h_ref[...], g_ref[...]).astype(jnp.bfloat16)
    cos = cos_ref[...]
    sin = sin_ref[...]
    lane = lax.broadcasted_iota(jnp.int32, cos.shape, 1)
    first_half = (lane & (HEAD_DIM // 2)) == 0

    def rope(t):
        rot = jnp.where(first_half, pltpu.roll(t, LANES - HEAD_DIM // 2, 1), pltpu.roll(t, HEAD_DIM // 2, 1))
        return t * cos + rot * sin

    q = _dot(hn, w_ref[:, 0:d])
    for c in range(d // LANES):
        sl = slice(c * LANES, (c + 1) * LANES)
        q_ref[:, sl] = (rope(q[:, sl]) * Q_SCALE).astype(q_ref.dtype)
    k = _dot(hn, w_ref[:, d:2 * d])
    for c in range(d // LANES):
        sl = slice(c * LANES, (c + 1) * LANES)
        kr = rope(k[:, sl])
        k_ref[:, sl] = kr.astype(k_ref.dtype)
        km_ref[0, :, sl] = jnp.mean(kr.reshape(tm // MOBA_BLOCK, MOBA_BLOCK, LANES), axis=1)
    v_ref[...] = _dot(hn, w_ref[:, 2 * d:3 * d]).astype(v_ref.dtype)


def _qkv_call(h, g, w_qkv, cos, sin, seq, layer, mixer):
    n, d = h.shape
    tm = ROW_TILE
    steps_per_seq = seq // tm
    row = lambda i: (i, 0)
    out_bf16 = jax.ShapeDtypeStruct((n, d), jnp.bfloat16)
    return pl.pallas_call(
        _qkv_kernel,
        grid=(n // tm,),
        in_specs=[pl.BlockSpec((tm, d), row), _layer_spec(layer, (1, d)), _layer_spec(mixer, (d, 3 * d)),
                  pl.BlockSpec((tm, LANES), lambda i: (i % steps_per_seq, 0)),
                  pl.BlockSpec((tm, LANES), lambda i: (i % steps_per_seq, 0))],
        out_specs=[pl.BlockSpec((tm, d), row)] * 3
        + [pl.BlockSpec((1, tm // MOBA_BLOCK, d), lambda i: (i, 0, 0))],
        out_shape=[out_bf16] * 3 + [jax.ShapeDtypeStruct((n // tm, tm // MOBA_BLOCK, d), jnp.float32)],
        compiler_params=_params(1),
        name="qkv_rope",
    )(h, g, w_qkv, cos, sin)


def _attn_kernel(q_ref, k_ref, v_ref, km_ref, o_ref, s_buf, p_buf, vt_buf):
    seq = q_ref.shape[0]
    blk = MOBA_BLOCK
    nb = seq // blk
    q_lane = lax.broadcasted_iota(jnp.int32, (blk, LANES), 1)
    causal = (lax.broadcasted_iota(jnp.int32, (blk, blk), 0) <= lax.broadcasted_iota(jnp.int32, (blk, blk), 1))
    out_row = lax.broadcasted_iota(jnp.int32, (LANES, blk), 0)
    km = km_ref[0]
    km_hi = km.astype(jnp.bfloat16)
    km_hl = jnp.concatenate([km_hi, (km - km_hi.astype(jnp.float32)).astype(jnp.bfloat16)], axis=0)

    def scores(i, hh, slot):
        n_keys = (i + 1) * blk
        q = q_ref[i * blk:(i + 1) * blk, :]
        in_head = (q_lane >= hh * HEAD_DIM) & (q_lane < (hh + 1) * HEAD_DIM)
        qm = jnp.where(in_head, q, jnp.zeros_like(q))
        s_t = lax.dot_general(k_ref[0:n_keys, :], qm, _NT, preferred_element_type=jnp.float32)
        pieces = []
        if i > MOBA_TOPK:
            gate = lax.dot_general(km_hl, qm, _NT, preferred_element_type=jnp.float32)
            gate = gate[0:nb, :] + gate[nb:2 * nb, :]
            rows = [gate[j:j + 1, :] for j in range(i)]
            for j in range(i):
                rank = jnp.zeros((1, blk), jnp.float32)
                for j2 in range(i):
                    if j2 == j:
                        continue
                    beats = (rows[j2] >= rows[j]) if j2 < j else (rows[j2] > rows[j])
                    rank = rank + jnp.where(beats, 1.0, 0.0)
                pieces.append(jnp.where(rank < MOBA_TOPK, s_t[j * blk:(j + 1) * blk, :], NEG_BIG))
        else:
            for j in range(i):
                pieces.append(s_t[j * blk:(j + 1) * blk, :])
        pieces.append(jnp.where(causal, s_t[i * blk:, :], NEG_BIG))
        m8 = None
        for j, piece in enumerate(pieces):
            s_buf[slot, j * blk:(j + 1) * blk, :] = piece
            pm8 = jnp.max(piece.reshape(blk // SUBLANES, SUBLANES, blk), axis=0)
            m8 = pm8 if m8 is None else jnp.maximum(m8, pm8)
        return jnp.max(m8, axis=0, keepdims=True)

    def probs(i, slot, m):
        for j in range(i + 1):
            t = (s_buf[slot, j * blk:(j + 1) * blk, :] - m).astype(p_buf.dtype)
            p_buf[slot, j * blk:(j + 1) * blk, :] = jnp.exp2(t)

    def weighted_values(i, slot):
        n_keys = (i + 1) * blk
        o_ext = _dot(vt_buf[:, 0:n_keys], p_buf[slot, 0:n_keys, :])
        return o_ext[0:LANES, :] * (1.0 / o_ext[LANES:LANES + 1, :])

    vt_buf[0:LANES, :] = v_ref[...].T
    vt_buf[LANES:LANES + ONES_ROWS, :] = jnp.ones((ONES_ROWS, seq), vt_buf.dtype)

    units = [(i, hh) for i in range(nb) for hh in range(HEADS_PER_LANE_TILE)]
    n_slots = s_buf.shape[0]
    ahead = n_slots - 1
    col_max = {n: scores(*units[n], n % n_slots) for n in range(min(ahead, len(units)))}
    head_out = []
    for n, (i, hh) in enumerate(units):
        if n + ahead < len(units):
            col_max[n + ahead] = scores(*units[n + ahead], (n + ahead) % n_slots)
        probs(i, n % n_slots, col_max.pop(n))
        head_out.append(weighted_values(i, n % n_slots))
        if hh == HEADS_PER_LANE_TILE - 1:
            o_t = jnp.where(out_row < HEAD_DIM, head_out[0], head_out[1])
            o_ref[i * blk:(i + 1) * blk, :] = o_t.T.astype(o_ref.dtype)
            head_out = []


def _attn_call(q, k, v, kmean, batch, seq):
    n, d = q.shape
    nb = seq // MOBA_BLOCK
    assert HEADS_PER_LANE_TILE == 2
    tile = lambda b, t: (b, t)
    return pl.pallas_call(
        _attn_kernel,
        grid=(batch, d // LANES),
        in_specs=[pl.BlockSpec((seq, LANES), tile)] * 3
        + [pl.BlockSpec((1, nb, LANES), lambda b, t: (b, 0, t))],
        out_specs=pl.BlockSpec((seq, LANES), tile),
        out_shape=jax.ShapeDtypeStruct((n, d), jnp.bfloat16),
        scratch_shapes=[pltpu.VMEM((ATTN_SLOTS, seq, MOBA_BLOCK), jnp.float32),
                        pltpu.VMEM((ATTN_SLOTS, seq, MOBA_BLOCK), jnp.bfloat16),
                        pltpu.VMEM((LANES + ONES_ROWS, seq), jnp.bfloat16)],
        compiler_params=_params(2, ATTN_FLAGS),
        name="moba_attn",
    )(q, k, v, kmean)


def _conv_kernel(h_ref, g_ref, w1_ref, b1_ref, wdw_ref, bdw_ref, lng_ref, lnb_ref, m_ref,
                 gbuf_ref, cbuf_ref, *, steps_per_seq):
    tm, d = h_ref.shape
    width = wdw_ref.shape[0]
    first_tap = CONV_HALO - (width - 1)
    seq_start = (pl.program_id(0) % steps_per_seq) == 0
    hn = _rmsnorm(h_ref[...], g_ref[...]).astype(jnp.bfloat16)
    a = _dot(hn, w1_ref[:, 0:d]) + b1_ref[:, 0:d]
    gate = _dot(hn, w1_ref[:, d:2 * d]) + b1_ref[:, d:2 * d]
    glu = a * jax.nn.sigmoid(gate)

    @pl.when(seq_start)
    def _():
        gbuf_ref[0:CONV_HALO, :] = jnp.zeros((CONV_HALO, d), jnp.float32)

    @pl.when(jnp.logical_not(seq_start))
    def _():
        gbuf_ref[0:CONV_HALO, :] = gbuf_ref[tm:tm + CONV_HALO, :]

    gbuf_ref[CONV_HALO:CONV_HALO + tm, :] = glu

    rc = CONV_ROW_CHUNK
    for lc in range(d // LANES):
        ls = slice(lc * LANES, (lc + 1) * LANES)

        def body(step, carry, ls=ls):
            r0 = pl.multiple_of(step * rc, rc)
            y = jnp.broadcast_to(bdw_ref[:, ls], (rc, LANES))
            for r in range(SUBLANES):
                rows = rc + (SUBLANES if r else 0)
                z = None
                for a_ in range(CONV_HALO // SUBLANES + 1):
                    o = SUBLANES * a_ + r
                    if o < first_tap or o > CONV_HALO:
                        continue
                    t = gbuf_ref[pl.ds(r0 + SUBLANES * a_, rows), ls] * wdw_ref[o - first_tap:o - first_tap + 1, ls]
                    z = t if z is None else z + t
                y = y + z[r:r + rc, :]
            cbuf_ref[pl.ds(r0, rc), ls] = y
            return carry

        lax.fori_loop(0, tm // rc, body, 0)

    c = cbuf_ref[...]
    mu = jnp.mean(c, axis=-1, keepdims=True)
    cc = c - mu
    var = jnp.mean(cc * cc, axis=-1, keepdims=True)
    y = cc * lax.rsqrt(var + LN_EPS) * lng_ref[...] + lnb_ref[...]
    m_ref[...] = (y * jax.nn.sigmoid(y)).astype(m_ref.dtype)


def _conv_call(h, g, w1, b1, wdw, bdw, lng, lnb, seq, layer, mixer):
    n, d = h.shape
    tm = ROW_TILE
    width = wdw.shape[1]
    assert width - 1 <= CONV_HALO and CONV_HALO % SUBLANES == 0
    row = lambda i: (i, 0)
    return pl.pallas_call(
        functools.partial(_conv_kernel, steps_per_seq=seq // tm),
        grid=(n // tm,),
        in_specs=[pl.BlockSpec((tm, d), row), _layer_spec(layer, (1, d)), _layer_spec(mixer, (d, 2 * d)),
                  _layer_spec(mixer, (1, 2 * d)), _layer_spec(mixer, (width, d)), _layer_spec(mixer, (1, d)),
                  _layer_spec(mixer, (1, d)), _layer_spec(mixer, (1, d))],
        out_specs=pl.BlockSpec((tm, d), row),
        out_shape=jax.ShapeDtypeStruct((n, d), jnp.bfloat16),
        scratch_shapes=[pltpu.VMEM((tm + CONV_HALO, d), jnp.float32), pltpu.VMEM((tm, d), jnp.float32)],
        compiler_params=_params(1),
        name="conformer_conv",
    )(h, g, w1, b1, wdw, bdw, lng, lnb)


def _ffn_kernel(h_ref, m_ref, wm_ref, bm_ref, g_ref, wup_ref, wdw_ref, bdw_ref, wdn_ref, gfin_ref, o_ref,
                carry_ref, *, steps_per_seq, final_norm):
    tm, d = h_ref.shape
    f = wdn_ref.shape[0]
    width = wdw_ref.shape[0]
    seq_start = (pl.program_id(0) % steps_per_seq) == 0
    h_mid = h_ref[...] + _dot(m_ref[...], wm_ref[...]) + bm_ref[...]
    hn = _rmsnorm(h_mid, g_ref[...]).astype(jnp.bfloat16)
    acc = h_mid
    off = 0
    for fn in FFN_CHUNKS:
        halves = []
        for part in range(2):
            cs = part * f + off
            u = _dot(hn, wup_ref[:, cs:cs + fn])
            prev = jnp.where(seq_start, 0.0, carry_ref[:, cs:cs + fn])
            carry_ref[:, cs:cs + fn] = u[tm - SUBLANES:, :]
            ext = jnp.concatenate([prev, u], axis=0)
            y = bdw_ref[:, cs:cs + fn] + u * wdw_ref[width - 1:width, cs:cs + fn]
            for j in range(width - 1):
                shift = width - 1 - j
                y = y + ext[SUBLANES - shift:SUBLANES - shift + tm, :] * wdw_ref[j:j + 1, cs:cs + fn]
            halves.append(y)
        act = (halves[0] * jax.nn.sigmoid(halves[0]) * halves[1]).astype(jnp.bfloat16)
        acc = acc + _dot(act, wdn_ref[off:off + fn, :])
        off += fn
    if final_norm:
        acc = _rmsnorm(acc, gfin_ref[...])
    o_ref[...] = acc


def _ffn_call(h, m, wm, bm, g, wup, wdw, bdw, wdn, gfin, seq, layer, mixer, final_norm):
    n, d = h.shape
    f = wdn.shape[1]
    tm = ROW_TILE
    width = wdw.shape[1]
    assert sum(FFN_CHUNKS) == f and width - 1 <= SUBLANES
    row = lambda i: (i, 0)
    return pl.pallas_call(
        functools.partial(_ffn_kernel, steps_per_seq=seq // tm, final_norm=final_norm),
        grid=(n // tm,),
        in_specs=[pl.BlockSpec((tm, d), row), pl.BlockSpec((tm, d), row), _layer_spec(mixer, (d, d)),
                  _layer_spec(mixer, (1, d)), _layer_spec(layer, (1, d)), _layer_spec(layer, (d, 2 * f)),
                  _layer_spec(layer, (width, 2 * f)), _layer_spec(layer, (1, 2 * f)), _layer_spec(layer, (f, d)),
                  _const_spec((1, d))],
        out_specs=pl.BlockSpec((tm, d), row),
        out_shape=jax.ShapeDtypeStruct((n, d), jnp.float32),
        scratch_shapes=[pltpu.VMEM((SUBLANES, 2 * f), jnp.float32)],
        compiler_params=_params(1),
        name="mix_proj_convffn",
    )(h, m, wm, bm, g, wup, wdw, bdw, wdn, gfin)


def _rope_tables(seq):
    pos = jnp.arange(seq, dtype=jnp.float32)
    inv_freq = 1.0 / (ROPE_THETA ** (jnp.arange(0, HEAD_DIM, 2, dtype=jnp.float32) / HEAD_DIM))
    ang = pos[:, None] * inv_freq[None, :]
    cos, sin = jnp.cos(ang), jnp.sin(ang)
    cos_t = jnp.tile(jnp.concatenate([cos, cos], axis=-1), (1, HEADS_PER_LANE_TILE))
    sin_t = jnp.tile(jnp.concatenate([-sin, sin], axis=-1), (1, HEADS_PER_LANE_TILE))
    return cos_t, sin_t


def kernel(x, norm_mix_g, norm_ffn_g, final_norm_g, attn_w_qkv, attn_w_o, conv_w_pw1, conv_b_pw1, conv_w_dw, conv_b_dw, conv_ln_g, conv_ln_b, conv_w_pw2, conv_b_pw2, ffn_w_up, ffn_w_dw, ffn_b_dw, ffn_w_down):
    batch, seq, d = x.shape
    depth = norm_mix_g.shape[0]
    bf16 = jnp.bfloat16
    cos_t, sin_t = _rope_tables(seq)
    row_vec = lambda a: a[:, None, :]
    g_mix, g_ffn = row_vec(norm_mix_g), row_vec(norm_ffn_g)
    w_qkv, w_o = attn_w_qkv.astype(bf16), attn_w_o.astype(bf16)
    b_o = jnp.zeros((attn_w_o.shape[0], 1, d), jnp.float32)
    w_pw1, w_pw2 = conv_w_pw1.astype(bf16), conv_w_pw2.astype(bf16)
    w_up, w_down = ffn_w_up.astype(bf16), ffn_w_down.astype(bf16)
    h = x.reshape(batch * seq, d)
    for i in range(depth):
        j = i // 2
        if i % 2 == 0:
            q, k, v, kmean = _qkv_call(h, g_mix, w_qkv, cos_t, sin_t, seq, i, j)
            kmean = kmean.reshape(batch, seq // MOBA_BLOCK, d)
            m = _attn_call(q, k, v, kmean, batch, seq)
            wm, bm = w_o, b_o
        else:
            m = _conv_call(h, g_mix, w_pw1, row_vec(conv_b_pw1), conv_w_dw, row_vec(conv_b_dw),
                           row_vec(conv_ln_g), row_vec(conv_ln_b), seq, i, j)
            wm, bm = w_pw2, row_vec(conv_b_pw2)
        h = _ffn_call(h, m, wm, bm, g_ffn, w_up, ffn_w_dw, row_vec(ffn_b_dw), w_down, final_norm_g[None, :],
                      seq, i, j, final_norm=(i == depth - 1))
    return h.reshape(batch, seq, d)
```

```python
import functools

import jax
import jax.numpy as jnp
from jax import lax
from jax.experimental import pallas as pl
from jax.experimental.pallas import tpu as pltpu

HEAD_DIM = 64
ROPE_THETA = 10000.0
MOBA_BLOCK = 256
MOBA_TOPK = 3
NORM_EPS = 1e-6
LN_EPS = 1e-5

LANES = 128
SUBLANES = 8
HEADS_PER_LANE_TILE = LANES // HEAD_DIM
VMEM_LIMIT_BYTES = 56 * 1024 * 1024

ROW_TILE = 512
FFN_CHUNKS = (1024, 1024, 768)
CONV_ROW_CHUNK = 128
CONV_HALO = 32
ATTN_SLOTS = 3
ATTN_LANE_TILES = 2
GATE_ROWS = 16
ONES_ROWS = 16
NEG_BIG = -1e30
Q_SCALE = HEAD_DIM ** -0.5 * 1.4426950408889634

_NT = (((1,), (1,)), ((), ()))


def _dot(a, b):
    return jnp.dot(a, b, preferred_element_type=jnp.float32)


def _rmsnorm(x, g):
    return x * lax.rsqrt(jnp.mean(x * x, axis=-1, keepdims=True) + NORM_EPS) * g


def _const_spec(shape):
    return pl.BlockSpec(shape, lambda *_: (0,) * len(shape), pipeline_mode=pl.Buffered(1))


def _layer_spec(layer, shape):
    return pl.BlockSpec((None,) + shape, lambda *_: (layer,) + (0,) * len(shape), pipeline_mode=pl.Buffered(1))


def _params(n_axes):
    return pltpu.CompilerParams(dimension_semantics=("arbitrary",) * n_axes,
                                vmem_limit_bytes=VMEM_LIMIT_BYTES)


def _qkv_kernel(h_ref, g_ref, w_ref, cos_ref, sin_ref, q_ref, k_ref, v_ref, km_ref):
    tm, d = h_ref.shape
    hn = _rmsnorm(h_ref[...], g_ref[...]).astype(jnp.bfloat16)
    cos = cos_ref[...]
    sin = sin_ref[...]
    lane = lax.broadcasted_iota(jnp.int32, cos.shape, 1)
    first_half = (lane & (HEAD_DIM // 2)) == 0

    def rope(t):
        rot = jnp.where(first_half, pltpu.roll(t, LANES - HEAD_DIM // 2, 1), pltpu.roll(t, HEAD_DIM // 2, 1))
        return t * cos + rot * sin

    q = _dot(hn, w_ref[:, 0:d])
    for c in range(d // LANES):
        sl = slice(c * LANES, (c + 1) * LANES)
        q_ref[:, sl] = (rope(q[:, sl]) * Q_SCALE).astype(q_ref.dtype)
    k = _dot(hn, w_ref[:, d:2 * d])
    for c in range(d // LANES):
        sl = slice(c * LANES, (c + 1) * LANES)
        kr = rope(k[:, sl])
        k_ref[:, sl] = kr.astype(k_ref.dtype)
        km_ref[0, :, sl] = jnp.mean(kr.reshape(tm // MOBA_BLOCK, MOBA_BLOCK, LANES), axis=1)
    v_ref[...] = _dot(hn, w_ref[:, 2 * d:3 * d]).astype(v_ref.dtype)


def _qkv_call(h, g, w_qkv, cos, sin, seq, layer, mixer):
    n, d = h.shape
    tm = ROW_TILE
    steps_per_seq = seq // tm
    row = lambda i: (i, 0)
    out_bf16 = jax.ShapeDtypeStruct((n, d), jnp.bfloat16)
    return pl.pallas_call(
        _qkv_kernel,
        grid=(n // tm,),
        in_specs=[pl.BlockSpec((tm, d), row), _layer_spec(layer, (1, d)), _layer_spec(mixer, (d, 3 * d)),
                  pl.BlockSpec((tm, LANES), lambda i: (i % steps_per_seq, 0)),
                  pl.BlockSpec((tm, LANES), lambda i: (i % steps_per_seq, 0))],
        out_specs=[pl.BlockSpec((tm, d), row)] * 3
        + [pl.BlockSpec((1, tm // MOBA_BLOCK, d), lambda i: (i, 0, 0))],
        out_shape=[out_bf16] * 3 + [jax.ShapeDtypeStruct((n // tm, tm // MOBA_BLOCK, d), jnp.float32)],
        compiler_params=_params(1),
        name="qkv_rope",
    )(h, g, w_qkv, cos, sin)


def _attn_kernel(q_ref, k_ref, v_ref, km_ref, o_ref, s_buf, p_buf, kx_buf, vt_buf):
    seq = q_ref.shape[0]
    blk = MOBA_BLOCK
    nb = seq // blk
    n_tiles = q_ref.shape[1] // LANES
    q_lane = lax.broadcasted_iota(jnp.int32, (blk, LANES), 1)
    causal = (lax.broadcasted_iota(jnp.int32, (blk, blk), 0) <= lax.broadcasted_iota(jnp.int32, (blk, blk), 1))
    out_row = lax.broadcasted_iota(jnp.int32, (LANES, blk), 0)

    for t in range(n_tiles):
        ls = slice(t * LANES, (t + 1) * LANES)
        km = km_ref[0, :, ls]
        km_hi = km.astype(jnp.bfloat16)
        kx_buf[t, 0:nb, :] = km_hi
        kx_buf[t, nb:GATE_ROWS, :] = (km - km_hi.astype(jnp.float32)).astype(jnp.bfloat16)
        kx_buf[t, GATE_ROWS:GATE_ROWS + seq, :] = k_ref[:, ls]
        vt_buf[t, 0:LANES, :] = v_ref[:, ls].T
        vt_buf[t, LANES:LANES + ONES_ROWS, :] = jnp.ones((ONES_ROWS, seq), vt_buf.dtype)

    def scores(t, i, hh, slot):
        n_keys = (i + 1) * blk
        q = q_ref[i * blk:(i + 1) * blk, t * LANES:(t + 1) * LANES]
        in_head = (q_lane >= hh * HEAD_DIM) & (q_lane < (hh + 1) * HEAD_DIM)
        qm = jnp.where(in_head, q, jnp.zeros_like(q))
        pieces = []
        if i > MOBA_TOPK:
            s_t = lax.dot_general(kx_buf[t, 0:GATE_ROWS + n_keys, :], qm, _NT, preferred_element_type=jnp.float32)
            gate = s_t[0:nb, :] + s_t[nb:GATE_ROWS, :]
            s_t = s_t[GATE_ROWS:, :]
            rows = [gate[j:j + 1, :] for j in range(i)]
            for j in range(i):
                rank = jnp.zeros((1, blk), jnp.float32)
                for j2 in range(i):
                    if j2 == j:
                        continue
                    beats = (rows[j2] >= rows[j]) if j2 < j else (rows[j2] > rows[j])
                    rank = rank + jnp.where(beats, 1.0, 0.0)
                pieces.append(jnp.where(rank < MOBA_TOPK, s_t[j * blk:(j + 1) * blk, :], NEG_BIG))
        else:
            s_t = lax.dot_general(kx_buf[t, GATE_ROWS:GATE_ROWS + n_keys, :], qm, _NT,
                                  preferred_element_type=jnp.float32)
            for j in range(i):
                pieces.append(s_t[j * blk:(j + 1) * blk, :])
        pieces.append(jnp.where(causal, s_t[i * blk:, :], NEG_BIG))
        m8 = None
        for j, piece in enumerate(pieces):
            s_buf[slot, j * blk:(j + 1) * blk, :] = piece
            pm8 = jnp.max(piece.reshape(blk // SUBLANES, SUBLANES, blk), axis=0)
            m8 = pm8 if m8 is None else jnp.maximum(m8, pm8)
        return jnp.max(m8, axis=0, keepdims=True)

    def probs(i, slot, m):
        for j in range(i + 1):
            arg = (s_buf[slot, j * blk:(j + 1) * blk, :] - m).astype(p_buf.dtype)
            p_buf[slot, j * blk:(j + 1) * blk, :] = jnp.exp2(arg)

    def weighted_values(t, i, slot):
        n_keys = (i + 1) * blk
        o_ext = _dot(vt_buf[t, :, 0:n_keys], p_buf[slot, 0:n_keys, :])
        return o_ext[0:LANES, :] * (1.0 / o_ext[LANES:LANES + 1, :])

    units = [(t, i, hh) for t in range(n_tiles) for i in range(nb) for hh in range(HEADS_PER_LANE_TILE)]
    n_slots = s_buf.shape[0]
    ahead = n_slots - 1
    col_max = {n: scores(*units[n], n % n_slots) for n in range(min(ahead, len(units)))}
    head_out = []
    for n, (t, i, hh) in enumerate(units):
        if n + ahead < len(units):
            col_max[n + ahead] = scores(*units[n + ahead], (n + ahead) % n_slots)
        probs(i, n % n_slots, col_max.pop(n))
        head_out.append(weighted_values(t, i, n % n_slots))
        if hh == HEADS_PER_LANE_TILE - 1:
            o_t = jnp.where(out_row < HEAD_DIM, head_out[0], head_out[1])
            o_ref[i * blk:(i + 1) * blk, t * LANES:(t + 1) * LANES] = o_t.T.astype(o_ref.dtype)
            head_out = []


def _attn_call(q, k, v, kmean, batch, seq):
    n, d = q.shape
    nb = seq // MOBA_BLOCK
    assert HEADS_PER_LANE_TILE == 2 and 2 * nb == GATE_ROWS
    width = ATTN_LANE_TILES * LANES
    tile = lambda b, t: (b, t)
    return pl.pallas_call(
        _attn_kernel,
        grid=(batch, d // width),
        in_specs=[pl.BlockSpec((seq, width), tile)] * 3
        + [pl.BlockSpec((1, nb, width), lambda b, t: (b, 0, t))],
        out_specs=pl.BlockSpec((seq, width), tile),
        out_shape=jax.ShapeDtypeStruct((n, d), jnp.bfloat16),
        scratch_shapes=[pltpu.VMEM((ATTN_SLOTS, seq, MOBA_BLOCK), jnp.float32),
                        pltpu.VMEM((ATTN_SLOTS, seq, MOBA_BLOCK), jnp.bfloat16),
                        pltpu.VMEM((ATTN_LANE_TILES, GATE_ROWS + seq, LANES), jnp.bfloat16),
                        pltpu.VMEM((ATTN_LANE_TILES, LANES + ONES_ROWS, seq), jnp.bfloat16)],
        compiler_params=_params(2),
        name="moba_attn",
    )(q, k, v, kmean)


def _conv_kernel(h_ref, g_ref, w1_ref, b1_ref, wdw_ref, bdw_ref, lng_ref, lnb_ref, m_ref,
                 gbuf_ref, cbuf_ref, *, steps_per_seq):
    tm, d = h_ref.shape
    width = wdw_ref.shape[0]
    first_tap = CONV_HALO - (width - 1)
    seq_start = (pl.program_id(0) % steps_per_seq) == 0
    hn = _rmsnorm(h_ref[...], g_ref[...]).astype(jnp.bfloat16)
    a = _dot(hn, w1_ref[:, 0:d]) + b1_ref[:, 0:d]
    gate = _dot(hn, w1_ref[:, d:2 * d]) + b1_ref[:, d:2 * d]
    glu = a * jax.nn.sigmoid(gate)

    @pl.when(seq_start)
    def _():
        gbuf_ref[0:CONV_HALO, :] = jnp.zeros((CONV_HALO, d), jnp.float32)

    @pl.when(jnp.logical_not(seq_start))
    def _():
        gbuf_ref[0:CONV_HALO, :] = gbuf_ref[tm:tm + CONV_HALO, :]

    gbuf_ref[CONV_HALO:CONV_HALO + tm, :] = glu

    rc = CONV_ROW_CHUNK
    for lc in range(d // LANES):
        ls = slice(lc * LANES, (lc + 1) * LANES)

        def body(step, carry, ls=ls):
            r0 = pl.multiple_of(step * rc, rc)
            y = jnp.broadcast_to(bdw_ref[:, ls], (rc, LANES))
            for r in range(SUBLANES):
                rows = rc + (SUBLANES if r else 0)
                z = None
                for a_ in range(CONV_HALO // SUBLANES + 1):
                    o = SUBLANES * a_ + r
                    if o < first_tap or o > CONV_HALO:
                        continue
                    t = gbuf_ref[pl.ds(r0 + SUBLANES * a_, rows), ls] * wdw_ref[o - first_tap:o - first_tap + 1, ls]
                    z = t if z is None else z + t
                y = y + z[r:r + rc, :]
            cbuf_ref[pl.ds(r0, rc), ls] = y
            return carry

        lax.fori_loop(0, tm // rc, body, 0)

    c = cbuf_ref[...]
    mu = jnp.mean(c, axis=-1, keepdims=True)
    cc = c - mu
    var = jnp.mean(cc * cc, axis=-1, keepdims=True)
    y = cc * lax.rsqrt(var + LN_EPS) * lng_ref[...] + lnb_ref[...]
    m_ref[...] = (y * jax.nn.sigmoid(y)).astype(m_ref.dtype)


def _conv_call(h, g, w1, b1, wdw, bdw, lng, lnb, seq, layer, mixer):
    n, d = h.shape
    tm = ROW_TILE
    width = wdw.shape[1]
    assert width - 1 <= CONV_HALO and CONV_HALO % SUBLANES == 0
    row = lambda i: (i, 0)
    return pl.pallas_call(
        functools.partial(_conv_kernel, steps_per_seq=seq // tm),
        grid=(n // tm,),
        in_specs=[pl.BlockSpec((tm, d), row), _layer_spec(layer, (1, d)), _layer_spec(mixer, (d, 2 * d)),
                  _layer_spec(mixer, (1, 2 * d)), _layer_spec(mixer, (width, d)), _layer_spec(mixer, (1, d)),
                  _layer_spec(mixer, (1, d)), _layer_spec(mixer, (1, d))],
        out_specs=pl.BlockSpec((tm, d), row),
        out_shape=jax.ShapeDtypeStruct((n, d), jnp.bfloat16),
        scratch_shapes=[pltpu.VMEM((tm + CONV_HALO, d), jnp.float32), pltpu.VMEM((tm, d), jnp.float32)],
        compiler_params=_params(1),
        name="conformer_conv",
    )(h, g, w1, b1, wdw, bdw, lng, lnb)


def _ffn_kernel(h_ref, m_ref, wm_ref, bm_ref, g_ref, wup_ref, wdw_ref, bdw_ref, wdn_ref, gfin_ref, o_ref,
                carry_ref, *, steps_per_seq, final_norm):
    tm, d = h_ref.shape
    f = wdn_ref.shape[0]
    width = wdw_ref.shape[0]
    seq_start = (pl.program_id(0) % steps_per_seq) == 0
    h_mid = h_ref[...] + _dot(m_ref[...], wm_ref[...]) + bm_ref[...]
    hn = _rmsnorm(h_mid, g_ref[...]).astype(jnp.bfloat16)
    offsets = [sum(FFN_CHUNKS[:c]) for c in range(len(FFN_CHUNKS))]

    def up_projection(c):
        fn, off = FFN_CHUNKS[c], offsets[c]
        return [_dot(hn, wup_ref[:, part * f + off:part * f + off + fn]) for part in range(2)]

    def activation(c, us):
        fn, off = FFN_CHUNKS[c], offsets[c]
        halves = []
        for part, u in enumerate(us):
            cs = part * f + off
            prev = jnp.where(seq_start, 0.0, carry_ref[:, cs:cs + fn])
            carry_ref[:, cs:cs + fn] = u[tm - SUBLANES:, :]
            ext = jnp.concatenate([prev, u], axis=0)
            y = bdw_ref[:, cs:cs + fn] + u * wdw_ref[width - 1:width, cs:cs + fn]
            for j in range(width - 1):
                shift = width - 1 - j
                y = y + ext[SUBLANES - shift:SUBLANES - shift + tm, :] * wdw_ref[j:j + 1, cs:cs + fn]
            halves.append(y)
        return (halves[0] * jax.nn.sigmoid(halves[0]) * halves[1]).astype(jnp.bfloat16)

    acc = h_mid
    us = up_projection(0)
    for c in range(len(FFN_CHUNKS)):
        us_next = up_projection(c + 1) if c + 1 < len(FFN_CHUNKS) else None
        acc = acc + _dot(activation(c, us), wdn_ref[offsets[c]:offsets[c] + FFN_CHUNKS[c], :])
        us = us_next
    o_ref[...] = _rmsnorm(acc, gfin_ref[...]) if final_norm else acc


def _ffn_call(h, m, wm, bm, g, wup, wdw, bdw, wdn, gfin, seq, layer, mixer, final_norm):
    n, d = h.shape
    f = wdn.shape[1]
    tm = ROW_TILE
    width = wdw.shape[1]
    assert sum(FFN_CHUNKS) == f and width - 1 <= SUBLANES
    row = lambda i: (i, 0)
    return pl.pallas_call(
        functools.partial(_ffn_kernel, steps_per_seq=seq // tm, final_norm=final_norm),
        grid=(n // tm,),
        in_specs=[pl.BlockSpec((tm, d), row), pl.BlockSpec((tm, d), row), _layer_spec(mixer, (d, d)),
                  _layer_spec(mixer, (1, d)), _layer_spec(layer, (1, d)), _layer_spec(layer, (d, 2 * f)),
                  _layer_spec(layer, (width, 2 * f)), _layer_spec(layer, (1, 2 * f)), _layer_spec(layer, (f, d)),
                  _const_spec((1, d))],
        out_specs=pl.BlockSpec((tm, d), row),
        out_shape=jax.ShapeDtypeStruct((n, d), jnp.float32),
        scratch_shapes=[pltpu.VMEM((SUBLANES, 2 * f), jnp.float32)],
        compiler_params=_params(1),
        name="mix_proj_convffn",
    )(h, m, wm, bm, g, wup, wdw, bdw, wdn, gfin)


def _rope_tables(seq):
    pos = jnp.arange(seq, dtype=jnp.float32)
    inv_freq = 1.0 / (ROPE_THETA ** (jnp.arange(0, HEAD_DIM, 2, dtype=jnp.float32) / HEAD_DIM))
    ang = pos[:, None] * inv_freq[None, :]
    cos, sin = jnp.cos(ang), jnp.sin(ang)
    cos_t = jnp.tile(jnp.concatenate([cos, cos], axis=-1), (1, HEADS_PER_LANE_TILE))
    sin_t = jnp.tile(jnp.concatenate([-sin, sin], axis=-1), (1, HEADS_PER_LANE_TILE))
    return cos_t, sin_t


def kernel(x, norm_mix_g, norm_ffn_g, final_norm_g, attn_w_qkv, attn_w_o, conv_w_pw1, conv_b_pw1, conv_w_dw, conv_b_dw, conv_ln_g, conv_ln_b, conv_w_pw2, conv_b_pw2, ffn_w_up, ffn_w_dw, ffn_b_dw, ffn_w_down):
    batch, seq, d = x.shape
    depth = norm_mix_g.shape[0]
    bf16 = jnp.bfloat16
    cos_t, sin_t = _rope_tables(seq)
    row_vec = lambda a: a[:, None, :]
    g_mix, g_ffn = row_vec(norm_mix_g), row_vec(norm_ffn_g)
    w_qkv, w_o = attn_w_qkv.astype(bf16), attn_w_o.astype(bf16)
    b_o = jnp.zeros((attn_w_o.shape[0], 1, d), jnp.float32)
    w_pw1, w_pw2 = conv_w_pw1.astype(bf16), conv_w_pw2.astype(bf16)
    w_up, w_down = ffn_w_up.astype(bf16), ffn_w_down.astype(bf16)
    h = x.reshape(batch * seq, d)
    for i in range(depth):
        j = i // 2
        if i % 2 == 0:
            q, k, v, kmean = _qkv_call(h, g_mix, w_qkv, cos_t, sin_t, seq, i, j)
            kmean = kmean.reshape(batch, seq // MOBA_BLOCK, d)
            m = _attn_call(q, k, v, kmean, batch, seq)
            wm, bm = w_o, b_o
        else:
            m = _conv_call(h, g_mix, w_pw1, row_vec(conv_b_pw1), conv_w_dw, row_vec(conv_b_dw),
                           row_vec(conv_ln_g), row_vec(conv_ln_b), seq, i, j)
            wm, bm = w_pw2, row_vec(conv_b_pw2)
        h = _ffn_call(h, m, wm, bm, g_ffn, w_up, ffn_w_dw, row_vec(ffn_b_dw), w_down, final_norm_g[None, :],
                      seq, i, j, final_norm=(i == depth - 1))
    return h.reshape(batch, seq, d)
```

```python
import functools

import jax
import jax.numpy as jnp
from jax import lax
from jax.experimental import pallas as pl
from jax.experimental.pallas import tpu as pltpu

HEAD_DIM = 64
ROPE_THETA = 10000.0
MOBA_BLOCK = 256
MOBA_TOPK = 3
NORM_EPS = 1e-6
LN_EPS = 1e-5

LANES = 128
SUBLANES = 8
HEADS_PER_LANE_TILE = LANES // HEAD_DIM
VMEM_LIMIT_BYTES = 56 * 1024 * 1024

ROW_TILE = 1024
FFN_ROW_TILE = 512
FFN_CHUNKS = (1024, 1024, 768)
CONV_ROW_CHUNK = 128
CONV_HALO = 32
ATTN_SLOTS = 3
ATTN_LANE_TILES = 2
GATE_ROWS = 16
ONES_ROWS = 16
NEG_BIG = -1e30
Q_SCALE = HEAD_DIM ** -0.5 * 1.4426950408889634

_NT = (((1,), (1,)), ((), ()))


def _dot(a, b):
    return jnp.dot(a, b, preferred_element_type=jnp.float32)


def _rmsnorm(x, g):
    return x * lax.rsqrt(jnp.mean(x * x, axis=-1, keepdims=True) + NORM_EPS) * g


def _const_spec(shape):
    return pl.BlockSpec(shape, lambda *_: (0,) * len(shape), pipeline_mode=pl.Buffered(1))


def _layer_spec(layer, shape):
    return pl.BlockSpec((None,) + shape, lambda *_: (layer,) + (0,) * len(shape), pipeline_mode=pl.Buffered(1))


def _params(n_axes):
    return pltpu.CompilerParams(dimension_semantics=("arbitrary",) * n_axes,
                                vmem_limit_bytes=VMEM_LIMIT_BYTES)


def _qkv_kernel(h_ref, g_ref, w_ref, cos_ref, sin_ref, q_ref, k_ref, v_ref, km_ref):
    tm, d = h_ref.shape
    hn = _rmsnorm(h_ref[...], g_ref[...]).astype(jnp.bfloat16)
    cos = cos_ref[...]
    sin = sin_ref[...]
    lane = lax.broadcasted_iota(jnp.int32, cos.shape, 1)
    first_half = (lane & (HEAD_DIM // 2)) == 0

    def rope(t):
        rot = jnp.where(first_half, pltpu.roll(t, LANES - HEAD_DIM // 2, 1), pltpu.roll(t, HEAD_DIM // 2, 1))
        return t * cos + rot * sin

    q = _dot(hn, w_ref[:, 0:d])
    for c in range(d // LANES):
        sl = slice(c * LANES, (c + 1) * LANES)
        q_ref[:, sl] = (rope(q[:, sl]) * Q_SCALE).astype(q_ref.dtype)
    k = _dot(hn, w_ref[:, d:2 * d])
    for c in range(d // LANES):
        sl = slice(c * LANES, (c + 1) * LANES)
        kr = rope(k[:, sl])
        k_ref[:, sl] = kr.astype(k_ref.dtype)
        km_ref[0, :, sl] = jnp.mean(kr.reshape(tm // MOBA_BLOCK, MOBA_BLOCK, LANES), axis=1)
    v_ref[...] = _dot(hn, w_ref[:, 2 * d:3 * d]).astype(v_ref.dtype)


def _qkv_call(h, g, w_qkv, cos, sin, seq, layer, mixer):
    n, d = h.shape
    tm = ROW_TILE
    steps_per_seq = seq // tm
    row = lambda i: (i, 0)
    out_bf16 = jax.ShapeDtypeStruct((n, d), jnp.bfloat16)
    return pl.pallas_call(
        _qkv_kernel,
        grid=(n // tm,),
        in_specs=[pl.BlockSpec((tm, d), row), _layer_spec(layer, (1, d)), _layer_spec(mixer, (d, 3 * d)),
                  pl.BlockSpec((tm, LANES), lambda i: (i % steps_per_seq, 0)),
                  pl.BlockSpec((tm, LANES), lambda i: (i % steps_per_seq, 0))],
        out_specs=[pl.BlockSpec((tm, d), row)] * 3
        + [pl.BlockSpec((1, tm // MOBA_BLOCK, d), lambda i: (i, 0, 0))],
        out_shape=[out_bf16] * 3 + [jax.ShapeDtypeStruct((n // tm, tm // MOBA_BLOCK, d), jnp.float32)],
        compiler_params=_params(1),
        name="qkv_rope",
    )(h, g, w_qkv, cos, sin)


def _attn_kernel(q_ref, k_ref, v_ref, km_ref, o_ref, s_buf, p_buf, kx_buf, vt_buf):
    seq = q_ref.shape[0]
    blk = MOBA_BLOCK
    nb = seq // blk
    n_tiles = q_ref.shape[1] // LANES
    q_lane = lax.broadcasted_iota(jnp.int32, (blk, LANES), 1)
    causal = (lax.broadcasted_iota(jnp.int32, (blk, blk), 0) <= lax.broadcasted_iota(jnp.int32, (blk, blk), 1))
    out_row = lax.broadcasted_iota(jnp.int32, (LANES, blk), 0)

    for t in range(n_tiles):
        ls = slice(t * LANES, (t + 1) * LANES)
        km = km_ref[0, :, ls]
        km_hi = km.astype(jnp.bfloat16)
        kx_buf[t, 0:nb, :] = km_hi
        kx_buf[t, nb:GATE_ROWS, :] = (km - km_hi.astype(jnp.float32)).astype(jnp.bfloat16)
        kx_buf[t, GATE_ROWS:GATE_ROWS + seq, :] = k_ref[:, ls]
        vt_buf[t, 0:LANES, :] = v_ref[:, ls].T
        vt_buf[t, LANES:LANES + ONES_ROWS, :] = jnp.ones((ONES_ROWS, seq), vt_buf.dtype)

    def scores(t, i, hh, slot):
        n_keys = (i + 1) * blk
        q = q_ref[i * blk:(i + 1) * blk, t * LANES:(t + 1) * LANES]
        in_head = (q_lane >= hh * HEAD_DIM) & (q_lane < (hh + 1) * HEAD_DIM)
        qm = jnp.where(in_head, q, jnp.zeros_like(q))
        pieces = []
        if i > MOBA_TOPK:
            s_t = lax.dot_general(kx_buf[t, 0:GATE_ROWS + n_keys, :], qm, _NT, preferred_element_type=jnp.float32)
            gate = s_t[0:nb, :] + s_t[nb:GATE_ROWS, :]
            s_t = s_t[GATE_ROWS:, :]
            rows = [gate[j:j + 1, :] for j in range(i)]
            for j in range(i):
                rank = jnp.zeros((1, blk), jnp.float32)
                for j2 in range(i):
                    if j2 == j:
                        continue
                    beats = (rows[j2] >= rows[j]) if j2 < j else (rows[j2] > rows[j])
                    rank = rank + jnp.where(beats, 1.0, 0.0)
                pieces.append(jnp.where(rank < MOBA_TOPK, s_t[j * blk:(j + 1) * blk, :], NEG_BIG))
        else:
            s_t = lax.dot_general(kx_buf[t, GATE_ROWS:GATE_ROWS + n_keys, :], qm, _NT,
                                  preferred_element_type=jnp.float32)
            for j in range(i):
                pieces.append(s_t[j * blk:(j + 1) * blk, :])
        pieces.append(jnp.where(causal, s_t[i * blk:, :], NEG_BIG))
        m8 = None
        for j, piece in enumerate(pieces):
            s_buf[slot, j * blk:(j + 1) * blk, :] = piece
            pm8 = jnp.max(piece.reshape(blk // SUBLANES, SUBLANES, blk), axis=0)
            m8 = pm8 if m8 is None else jnp.maximum(m8, pm8)
        return jnp.max(m8, axis=0, keepdims=True)

    def probs(i, slot, m):
        for j in range(i + 1):
            arg = (s_buf[slot, j * blk:(j + 1) * blk, :] - m).astype(p_buf.dtype)
            p_buf[slot, j * blk:(j + 1) * blk, :] = jnp.exp2(arg)

    def weighted_values(t, i, slot):
        n_keys = (i + 1) * blk
        o_ext = _dot(vt_buf[t, :, 0:n_keys], p_buf[slot, 0:n_keys, :])
        return o_ext[0:LANES, :] * (1.0 / o_ext[LANES:LANES + 1, :])

    units = [(t, i, hh) for t in range(n_tiles) for i in range(nb) for hh in range(HEADS_PER_LANE_TILE)]
    n_slots = s_buf.shape[0]
    ahead = n_slots - 1
    col_max = {n: scores(*units[n], n % n_slots) for n in range(min(ahead, len(units)))}
    head_out = []
    for n, (t, i, hh) in enumerate(units):
        if n + ahead < len(units):
            col_max[n + ahead] = scores(*units[n + ahead], (n + ahead) % n_slots)
        probs(i, n % n_slots, col_max.pop(n))
        head_out.append(weighted_values(t, i, n % n_slots))
        if hh == HEADS_PER_LANE_TILE - 1:
            o_t = jnp.where(out_row < HEAD_DIM, head_out[0], head_out[1])
            o_ref[i * blk:(i + 1) * blk, t * LANES:(t + 1) * LANES] = o_t.T.astype(o_ref.dtype)
            head_out = []


def _attn_call(q, k, v, kmean, batch, seq):
    n, d = q.shape
    nb = seq // MOBA_BLOCK
    assert HEADS_PER_LANE_TILE == 2 and 2 * nb == GATE_ROWS
    width = ATTN_LANE_TILES * LANES
    tile = lambda b, t: (b, t)
    return pl.pallas_call(
        _attn_kernel,
        grid=(batch, d // width),
        in_specs=[pl.BlockSpec((seq, width), tile)] * 3
        + [pl.BlockSpec((1, nb, width), lambda b, t: (b, 0, t))],
        out_specs=pl.BlockSpec((seq, width), tile),
        out_shape=jax.ShapeDtypeStruct((n, d), jnp.bfloat16),
        scratch_shapes=[pltpu.VMEM((ATTN_SLOTS, seq, MOBA_BLOCK), jnp.float32),
                        pltpu.VMEM((ATTN_SLOTS, seq, MOBA_BLOCK), jnp.bfloat16),
                        pltpu.VMEM((ATTN_LANE_TILES, GATE_ROWS + seq, LANES), jnp.bfloat16),
                        pltpu.VMEM((ATTN_LANE_TILES, LANES + ONES_ROWS, seq), jnp.bfloat16)],
        compiler_params=_params(2),
        name="moba_attn",
    )(q, k, v, kmean)


def _conv_kernel(h_ref, g_ref, w1_ref, b1_ref, wdw_ref, bdw_ref, lng_ref, lnb_ref, m_ref,
                 gbuf_ref, cbuf_ref, *, steps_per_seq):
    tm, d = h_ref.shape
    width = wdw_ref.shape[0]
    first_tap = CONV_HALO - (width - 1)
    seq_start = (pl.program_id(0) % steps_per_seq) == 0
    hn = _rmsnorm(h_ref[...], g_ref[...]).astype(jnp.bfloat16)
    a = _dot(hn, w1_ref[:, 0:d]) + b1_ref[:, 0:d]
    gate = _dot(hn, w1_ref[:, d:2 * d]) + b1_ref[:, d:2 * d]
    glu = a * jax.nn.sigmoid(gate)

    @pl.when(seq_start)
    def _():
        gbuf_ref[0:CONV_HALO, :] = jnp.zeros((CONV_HALO, d), jnp.float32)

    @pl.when(jnp.logical_not(seq_start))
    def _():
        gbuf_ref[0:CONV_HALO, :] = gbuf_ref[tm:tm + CONV_HALO, :]

    gbuf_ref[CONV_HALO:CONV_HALO + tm, :] = glu

    rc = CONV_ROW_CHUNK
    for lc in range(d // LANES):
        ls = slice(lc * LANES, (lc + 1) * LANES)

        def body(step, carry, ls=ls):
            r0 = pl.multiple_of(step * rc, rc)
            y = jnp.broadcast_to(bdw_ref[:, ls], (rc, LANES))
            for r in range(SUBLANES):
                rows = rc + (SUBLANES if r else 0)
                z = None
                for a_ in range(CONV_HALO // SUBLANES + 1):
                    o = SUBLANES * a_ + r
                    if o < first_tap or o > CONV_HALO:
                        continue
                    t = gbuf_ref[pl.ds(r0 + SUBLANES * a_, rows), ls] * wdw_ref[o - first_tap:o - first_tap + 1, ls]
                    z = t if z is None else z + t
                y = y + z[r:r + rc, :]
            cbuf_ref[pl.ds(r0, rc), ls] = y
            return carry

        lax.fori_loop(0, tm // rc, body, 0)

    c = cbuf_ref[...]
    mu = jnp.mean(c, axis=-1, keepdims=True)
    cc = c - mu
    var = jnp.mean(cc * cc, axis=-1, keepdims=True)
    y = cc * lax.rsqrt(var + LN_EPS) * lng_ref[...] + lnb_ref[...]
    m_ref[...] = (y * jax.nn.sigmoid(y)).astype(m_ref.dtype)


def _conv_call(h, g, w1, b1, wdw, bdw, lng, lnb, seq, layer, mixer):
    n, d = h.shape
    tm = ROW_TILE
    width = wdw.shape[1]
    assert width - 1 <= CONV_HALO and CONV_HALO % SUBLANES == 0
    row = lambda i: (i, 0)
    return pl.pallas_call(
        functools.partial(_conv_kernel, steps_per_seq=seq // tm),
        grid=(n // tm,),
        in_specs=[pl.BlockSpec((tm, d), row), _layer_spec(layer, (1, d)), _layer_spec(mixer, (d, 2 * d)),
                  _layer_spec(mixer, (1, 2 * d)), _layer_spec(mixer, (width, d)), _layer_spec(mixer, (1, d)),
                  _layer_spec(mixer, (1, d)), _layer_spec(mixer, (1, d))],
        out_specs=pl.BlockSpec((tm, d), row),
        out_shape=jax.ShapeDtypeStruct((n, d), jnp.bfloat16),
        scratch_shapes=[pltpu.VMEM((tm + CONV_HALO, d), jnp.float32), pltpu.VMEM((tm, d), jnp.float32)],
        compiler_params=_params(1),
        name="conformer_conv",
    )(h, g, w1, b1, wdw, bdw, lng, lnb)


def _ffn_kernel(h_ref, m_ref, wm_ref, bm_ref, g_ref, wup_ref, wdw_ref, bdw_ref, wdn_ref, gfin_ref, o_ref,
                carry_ref, *, steps_per_seq, final_norm):
    tm, d = h_ref.shape
    f = wdn_ref.shape[0]
    width = wdw_ref.shape[0]
    seq_start = (pl.program_id(0) % steps_per_seq) == 0
    h_mid = h_ref[...] + _dot(m_ref[...], wm_ref[...]) + bm_ref[...]
    hn = _rmsnorm(h_mid, g_ref[...]).astype(jnp.bfloat16)
    offsets = [sum(FFN_CHUNKS[:c]) for c in range(len(FFN_CHUNKS))]

    def up_projection(c):
        fn, off = FFN_CHUNKS[c], offsets[c]
        return [_dot(hn, wup_ref[:, part * f + off:part * f + off + fn]) for part in range(2)]

    def activation(c, us):
        fn, off = FFN_CHUNKS[c], offsets[c]
        halves = []
        for part, u in enumerate(us):
            cs = part * f + off
            prev = jnp.where(seq_start, 0.0, carry_ref[:, cs:cs + fn])
            carry_ref[:, cs:cs + fn] = u[tm - SUBLANES:, :]
            ext = jnp.concatenate([prev, u], axis=0)
            y = bdw_ref[:, cs:cs + fn] + u * wdw_ref[width - 1:width, cs:cs + fn]
            for j in range(width - 1):
                shift = width - 1 - j
                y = y + ext[SUBLANES - shift:SUBLANES - shift + tm, :] * wdw_ref[j:j + 1, cs:cs + fn]
            halves.append(y)
        return (halves[0] * jax.nn.sigmoid(halves[0]) * halves[1]).astype(jnp.bfloat16)

    acc = h_mid
    us = up_projection(0)
    for c in range(len(FFN_CHUNKS)):
        us_next = up_projection(c + 1) if c + 1 < len(FFN_CHUNKS) else None
        acc = acc + _dot(activation(c, us), wdn_ref[offsets[c]:offsets[c] + FFN_CHUNKS[c], :])
        us = us_next
    o_ref[...] = _rmsnorm(acc, gfin_ref[...]) if final_norm else acc


def _ffn_call(h, m, wm, bm, g, wup, wdw, bdw, wdn, gfin, seq, layer, mixer, final_norm):
    n, d = h.shape
    f = wdn.shape[1]
    tm = FFN_ROW_TILE
    width = wdw.shape[1]
    assert sum(FFN_CHUNKS) == f and width - 1 <= SUBLANES
    row = lambda i: (i, 0)
    return pl.pallas_call(
        functools.partial(_ffn_kernel, steps_per_seq=seq // tm, final_norm=final_norm),
        grid=(n // tm,),
        in_specs=[pl.BlockSpec((tm, d), row), pl.BlockSpec((tm, d), row), _layer_spec(mixer, (d, d)),
                  _layer_spec(mixer, (1, d)), _layer_spec(layer, (1, d)), _layer_spec(layer, (d, 2 * f)),
                  _layer_spec(layer, (width, 2 * f)), _layer_spec(layer, (1, 2 * f)), _layer_spec(layer, (f, d)),
                  _const_spec((1, d))],
        out_specs=pl.BlockSpec((tm, d), row),
        out_shape=jax.ShapeDtypeStruct((n, d), jnp.float32),
        scratch_shapes=[pltpu.VMEM((SUBLANES, 2 * f), jnp.float32)],
        compiler_params=_params(1),
        name="mix_proj_convffn",
    )(h, m, wm, bm, g, wup, wdw, bdw, wdn, gfin)


def _rope_tables(seq):
    pos = jnp.arange(seq, dtype=jnp.float32)
    inv_freq = 1.0 / (ROPE_THETA ** (jnp.arange(0, HEAD_DIM, 2, dtype=jnp.float32) / HEAD_DIM))
    ang = pos[:, None] * inv_freq[None, :]
    cos, sin = jnp.cos(ang), jnp.sin(ang)
    cos_t = jnp.tile(jnp.concatenate([cos, cos], axis=-1), (1, HEADS_PER_LANE_TILE))
    sin_t = jnp.tile(jnp.concatenate([-sin, sin], axis=-1), (1, HEADS_PER_LANE_TILE))
    return cos_t, sin_t


def kernel(x, norm_mix_g, norm_ffn_g, final_norm_g, attn_w_qkv, attn_w_o, conv_w_pw1, conv_b_pw1, conv_w_dw, conv_b_dw, conv_ln_g, conv_ln_b, conv_w_pw2, conv_b_pw2, ffn_w_up, ffn_w_dw, ffn_b_dw, ffn_w_down):
    batch, seq, d = x.shape
    depth = norm_mix_g.shape[0]
    bf16 = jnp.bfloat16
    cos_t, sin_t = _rope_tables(seq)
    row_vec = lambda a: a[:, None, :]
    g_mix, g_ffn = row_vec(norm_mix_g), row_vec(norm_ffn_g)
    w_qkv, w_o = attn_w_qkv.astype(bf16), attn_w_o.astype(bf16)
    b_o = jnp.zeros((attn_w_o.shape[0], 1, d), jnp.float32)
    w_pw1, w_pw2 = conv_w_pw1.astype(bf16), conv_w_pw2.astype(bf16)
    w_up, w_down = ffn_w_up.astype(bf16), ffn_w_down.astype(bf16)
    h = x.reshape(batch * seq, d)
    for i in range(depth):
        j = i // 2
        if i % 2 == 0:
            q, k, v, kmean = _qkv_call(h, g_mix, w_qkv, cos_t, sin_t, seq, i, j)
            kmean = kmean.reshape(batch, seq // MOBA_BLOCK, d)
            m = _attn_call(q, k, v, kmean, batch, seq)
            wm, bm = w_o, b_o
        else:
            m = _conv_call(h, g_mix, w_pw1, row_vec(conv_b_pw1), conv_w_dw, row_vec(conv_b_dw),
                           row_vec(conv_ln_g), row_vec(conv_ln_b), seq, i, j)
            wm, bm = w_pw2, row_vec(conv_b_pw2)
        h = _ffn_call(h, m, wm, bm, g_ffn, w_up, ffn_w_dw, row_vec(ffn_b_dw), w_down, final_norm_g[None, :],
                      seq, i, j, final_norm=(i == depth - 1))
    return h.reshape(batch, seq, d)
```
